```python
import math
import jax, jax.numpy as jnp
from jax import lax
import numpy as np

D_MODEL = 2048
BATCH = 8
SEQ = 2048
DEPTH = 2
DEC_BATCH = 128
DEC_SEQ = 4
PAST_LEN = 2048
PAGE_SIZE = 128

CHUNK = 128
D_INNER = D_MODEL
N_GROUPS = 16
GROUP_DIM = D_INNER // N_GROUPS
HEAD_DIM = 128
V_DIM = 2 * HEAD_DIM
N_HEADS = D_MODEL // (2 * HEAD_DIM)
QK_WIDTH = N_HEADS * 2 * HEAD_DIM
ROPE_THETA = 10000.0
Q_BLOCK = 128
D_FF = (11 * D_MODEL) // 4
N_EXPERTS = 8
TOP_K = 2
EPS = 1e-6
N_EVEN = (DEPTH + 1) // 2
N_ODD = DEPTH // 2

kernel_name = "hybrid_chunkgmlp_diffattn_moe_step"

F32 = jnp.float32


def rms_norm(x, g):
    xf = x.astype(F32)
    y = xf * lax.rsqrt(jnp.mean(xf * xf, axis=-1, keepdims=True) + EPS)
    return (y * g.astype(F32)).astype(x.dtype)


def layer_norm(x, g, b):
    xf = x.astype(F32)
    mu = jnp.mean(xf, axis=-1, keepdims=True)
    var = jnp.mean(jnp.square(xf - mu), axis=-1, keepdims=True)
    y = (xf - mu) * lax.rsqrt(var + EPS)
    return (y * g.astype(F32) + b.astype(F32)).astype(x.dtype)


def rope(x, pos):
    inv = jnp.exp(-math.log(ROPE_THETA) * jnp.arange(0, HEAD_DIM, 2, dtype=F32) / HEAD_DIM)
    ang = pos.astype(F32)[:, None] * inv[None, :]
    ang = jnp.concatenate([ang, ang], axis=-1)[:, None, None, :]
    cos, sin = jnp.cos(ang), jnp.sin(ang)
    xf = x.astype(F32)
    x1, x2 = xf[..., :HEAD_DIM // 2], xf[..., HEAD_DIM // 2:]
    rot = jnp.concatenate([-x2, x1], axis=-1)
    return (xf * cos + rot * sin).astype(x.dtype)


def diff_attend(q, k, v, qpos, kpos, lam):
    s = jnp.einsum('...qhcd,...khcd->...chqk', q, k).astype(F32) * (HEAD_DIM ** -0.5)
    mask = kpos[None, :] <= qpos[:, None]
    s = jnp.where(mask, s, -1e30)
    p = jax.nn.softmax(s, axis=-1)
    a = p[..., 0, :, :, :] - lam * p[..., 1, :, :, :]
    return jnp.einsum('...hqk,...khe->...qhe', a.astype(v.dtype), v)


def attend_prompt(q, k, v, lam):
    b, s = q.shape[0], q.shape[1]
    nb = s // Q_BLOCK
    qb = jnp.moveaxis(q.reshape(b, nb, Q_BLOCK, N_HEADS, 2, HEAD_DIM), 1, 0)
    kpos = jnp.arange(s)

    def block(args):
        qi, bi = args
        qpos = bi * Q_BLOCK + jnp.arange(Q_BLOCK)
        return diff_attend(qi, k, v, qpos, kpos, lam)

    o = lax.map(block, (qb, jnp.arange(nb)))
    return jnp.moveaxis(o, 0, 1).reshape(b, s, N_HEADS, V_DIM)


def gmlp_mixer(h, w_in, ln_g, ln_b, w_s, b_s, w_out):
    b, t, _ = h.shape
    uv = jax.nn.gelu(h @ w_in)
    u, v = uv[..., :D_INNER], uv[..., D_INNER:]
    v = layer_norm(v, ln_g, ln_b)
    pad = (-t) % CHUNK
    nc = (t + pad) // CHUNK
    vg = jnp.pad(v, ((0, 0), (0, pad), (0, 0))).reshape(b, nc, CHUNK, N_GROUPS, GROUP_DIM)
    causal = jnp.tril(jnp.ones((CHUNK, CHUNK), dtype=bool))
    w = jnp.where(causal[None], w_s, 0)
    z = jnp.einsum('gts,bnsgc->bntgc', w, vg) + jnp.transpose(b_s)[:, :, None]
    z = z.reshape(b, nc * CHUNK, D_INNER)[:, :t]
    return (u * z) @ w_out, v


def swiglu(h, w_gu, w_down):
    gu = h @ w_gu
    g, u = gu[..., :D_FF], gu[..., D_FF:]
    return (jax.nn.silu(g) * u) @ w_down


def moe_swiglu(h, w_r, b_r, w_gu, w_down):
    logits = (h @ w_r).astype(F32) + b_r.astype(F32)
    top_v, top_i = lax.top_k(logits, TOP_K)
    gates = jax.nn.softmax(top_v, axis=-1)
    dense_gate = jnp.sum(jax.nn.one_hot(top_i, N_EXPERTS, dtype=F32) * gates[..., None], axis=-2)
    out = jnp.zeros_like(h)
    for e in range(N_EXPERTS):
        out = out + dense_gate[..., e:e + 1].astype(h.dtype) * swiglu(h, w_gu[e], w_down[e])
    return out


def setup_inputs(seed: int = 0) -> dict:
    key = jax.random.key(seed)
    ks = jax.random.split(key, 24)
    n_pages = PAST_LEN // PAGE_SIZE
    n_used = DEC_BATCH * n_pages
    n_pool = n_used + n_used // 4
    nrm = jax.random.normal
    page_table = jax.random.permutation(ks[4], n_pool)[:n_used].reshape(DEC_BATCH, n_pages).astype(jnp.int32)
    return {
        "x_prompt": nrm(ks[0], (BATCH, SEQ, D_MODEL), F32),
        "x_sample": nrm(ks[1], (DEC_BATCH, DEC_SEQ, D_MODEL), F32),
        "cache_k": nrm(ks[2], (N_ODD, n_pool, PAGE_SIZE, N_HEADS, 2, HEAD_DIM), F32),
        "cache_v": nrm(ks[3], (N_ODD, n_pool, PAGE_SIZE, N_HEADS, V_DIM), F32),
        "page_table": page_table,
        "norm_g": 1.0 + 0.02 * nrm(ks[5], (DEPTH, 4, D_MODEL), F32),
        "a_w_in": nrm(ks[6], (N_EVEN, D_MODEL, 2 * D_INNER), F32) * D_MODEL ** -0.5,
        "a_ln_g": 1.0 + 0.02 * nrm(ks[7], (N_EVEN, D_INNER), F32),
        "a_ln_b": 0.02 * nrm(ks[8], (N_EVEN, D_INNER), F32),
        "a_w_s": nrm(ks[9], (N_EVEN, N_GROUPS, CHUNK, CHUNK), F32) * CHUNK ** -0.5,
        "a_b_s": 1.0 + 0.02 * nrm(ks[10], (N_EVEN, N_GROUPS, CHUNK), F32),
        "a_w_out": nrm(ks[11], (N_EVEN, D_INNER, D_MODEL), F32) * D_INNER ** -0.5,
        "f_w_gu": nrm(ks[12], (N_EVEN, D_MODEL, 2 * D_FF), F32) * D_MODEL ** -0.5,
        "f_w_down": nrm(ks[13], (N_EVEN, D_FF, D_MODEL), F32) * D_FF ** -0.5,
        "b_w_qkv": nrm(ks[14], (N_ODD, D_MODEL, 2 * QK_WIDTH + N_HEADS * V_DIM), F32) * D_MODEL ** -0.5,
        "b_lam": 0.1 * nrm(ks[15], (N_ODD, 4, HEAD_DIM), F32),
        "b_subln": 1.0 + 0.02 * nrm(ks[16], (N_ODD, V_DIM), F32),
        "b_w_o": nrm(ks[17], (N_ODD, N_HEADS * V_DIM, D_MODEL), F32) * (N_HEADS * V_DIM) ** -0.5,
        "e_w_r": nrm(ks[18], (N_ODD, D_MODEL, N_EXPERTS), F32) * D_MODEL ** -0.5,
        "e_b_r": 0.01 * nrm(ks[19], (N_ODD, N_EXPERTS), F32),
        "e_w_gu": nrm(ks[20], (N_ODD, N_EXPERTS, D_MODEL, 2 * D_FF), F32) * D_MODEL ** -0.5,
        "e_w_down": nrm(ks[21], (N_ODD, N_EXPERTS, D_FF, D_MODEL), F32) * D_FF ** -0.5,
    }


def reference(x_prompt, x_sample, cache_k, cache_v, page_table, norm_g, a_w_in, a_ln_g, a_ln_b, a_w_s, a_b_s, a_w_out, f_w_gu, f_w_down, b_w_qkv, b_lam, b_subln, b_w_o, e_w_r, e_b_r, e_w_gu, e_w_down):
    past = page_table.shape[1] * PAGE_SIZE

    def attend_sample(j, q, k_new, v_new, lam):
        t = q.shape[1]
        qpos = past + jnp.arange(t)
        kpos = jnp.arange(past + t)

        def one_seq(args):
            qs, kn, vn, pt = args
            kp = cache_k[j, pt].reshape(past, N_HEADS, 2, HEAD_DIM)
            vp = cache_v[j, pt].reshape(past, N_HEADS, V_DIM)
            kk = jnp.concatenate([kp, kn.astype(kp.dtype)], axis=0)
            vv = jnp.concatenate([vp, vn.astype(vp.dtype)], axis=0)
            return diff_attend(qs, kk, vv, qpos, kpos, lam)

        return lax.map(one_seq, (q, k_new, v_new, page_table))

    def attend_full(j, q, k, v, lam):
        return attend_prompt(q, k, v, lam)

    def diff_attn_layer(h, pos, i, j, attend):
        b, t, _ = h.shape
        qkv = h @ b_w_qkv[j]
        q = rope(qkv[..., :QK_WIDTH].reshape(b, t, N_HEADS, 2, HEAD_DIM), pos)
        k = rope(qkv[..., QK_WIDTH:2 * QK_WIDTH].reshape(b, t, N_HEADS, 2, HEAD_DIM), pos)
        v = qkv[..., 2 * QK_WIDTH:].reshape(b, t, N_HEADS, V_DIM)
        lam_init = 0.8 - 0.6 * math.exp(-0.3 * i)
        lp = b_lam[j].astype(F32)
        lam = jnp.exp(jnp.sum(lp[0] * lp[1])) - jnp.exp(jnp.sum(lp[2] * lp[3])) + lam_init
        o = attend(j, q, k, v, lam)
        o = rms_norm(o, b_subln[j]) * (1.0 - lam_init)
        return o.reshape(b, t, N_HEADS * V_DIM) @ b_w_o[j], k, v

    def trunk(x, pos, attend):
        chunk_v, ks, vs = [], [], []
        for i in range(DEPTH):
            j = i // 2
            h = rms_norm(x, norm_g[i, 0])
            if i % 2 == 0:
                m, v_rows = gmlp_mixer(h, a_w_in[j], a_ln_g[j], a_ln_b[j], a_w_s[j], a_b_s[j], a_w_out[j])
                chunk_v.append(v_rows)
            else:
                m, k, v = diff_attn_layer(h, pos, i, j, attend)
                ks.append(k)
                vs.append(v)
            x = x + rms_norm(m, norm_g[i, 1])
            h = rms_norm(x, norm_g[i, 2])
            if i % 2 == 0:
                f = swiglu(h, f_w_gu[j], f_w_down[j])
            else:
                f = moe_swiglu(h, e_w_r[j], e_b_r[j], e_w_gu[j], e_w_down[j])
            x = x + rms_norm(f, norm_g[i, 3])
        return x, jnp.stack(chunk_v), jnp.stack(ks), jnp.stack(vs)

    pos_prompt = jnp.arange(x_prompt.shape[1])
    y_prompt, _, new_k_prompt, new_v_prompt = trunk(x_prompt, pos_prompt, attend_full)
    pos_sample = past + jnp.arange(x_sample.shape[1])
    y_sample, new_chunk_v_sample, new_k_sample, new_v_sample = trunk(x_sample, pos_sample, attend_sample)
    return (y_prompt, y_sample, new_k_prompt, new_v_prompt, new_k_sample, new_v_sample, new_chunk_v_sample)
```

```python
import functools
import math

import jax
import jax.numpy as jnp
from jax import lax
from jax.experimental import pallas as pl
from jax.experimental.pallas import tpu as pltpu

F32 = jnp.float32
BF16 = jnp.bfloat16
I32 = jnp.int32

EPS = 1e-6
CHUNK = 128
N_GROUPS = 16
HEAD_DIM = 128
ROPE_THETA = 10000.0
TOP_K = 2
NEG = -1e30
LANES = 128
SUBLANES = 8
VMEM_LIMIT = 56 * 1024 * 1024


def _dot(a, b):
    return jnp.dot(a, b, preferred_element_type=F32)


def _dot_nt(a, b):
    return lax.dot_general(a, b, (((1,), (1,)), ((), ())), preferred_element_type=F32)


def _rms(x, g):
    return x * lax.rsqrt(jnp.mean(x * x, axis=-1, keepdims=True) + EPS) * g


def _tile(n, pref):
    t = min(n, pref)
    while n % t:
        t -= SUBLANES
    assert t > 0 and t % SUBLANES == 0, (n, pref)
    return t


def _params(sem, vmem=VMEM_LIMIT):
    return pltpu.CompilerParams(dimension_semantics=sem, vmem_limit_bytes=vmem)


def _resident(shape):
    return pl.BlockSpec(shape, lambda *_: (0,) * len(shape), pipeline_mode=pl.Buffered(1))


def _gmlp_kernel(x_ref, g_ref, win_ref, lng_ref, lnb_ref, wmix_ref, bias_ref, wout_ref,
                 out_ref, v_ref, h_scr, u_scr, vb_scr, y_scr, *, col_w):
    tm, d = x_ref.shape
    d_inner = u_scr.shape[1]
    x = x_ref[...]
    h_scr[...] = _rms(x, g_ref[0:1, :]).astype(BF16)
    n_col = d_inner // col_w
    s1 = jnp.zeros((tm, 1), F32)
    for c in range(n_col):
        cs = slice(c * col_w, (c + 1) * col_w)
        u_scr[:, cs] = jax.nn.gelu(_dot(h_scr[...], win_ref[:, cs]))
        v = jax.nn.gelu(_dot(h_scr[...], win_ref[:, d_inner + c * col_w:d_inner + (c + 1) * col_w]))
        v_ref[:, cs] = v
        s1 = s1 + jnp.sum(v, axis=-1, keepdims=True)
    mu = s1 / d_inner
    s2 = jnp.zeros((tm, 1), F32)
    for c in range(n_col):
        cs = slice(c * col_w, (c + 1) * col_w)
        dv = v_ref[:, cs] - mu
        s2 = s2 + jnp.sum(dv * dv, axis=-1, keepdims=True)
    rstd = lax.rsqrt(s2 / d_inner + EPS)
    for c in range(n_col):
        cs = slice(c * col_w, (c + 1) * col_w)
        vn = (v_ref[:, cs] - mu) * rstd * lng_ref[:, cs] + lnb_ref[:, cs]
        v_ref[:, cs] = vn
        vb_scr[:, cs] = vn.astype(BF16)
    gd = d_inner // N_GROUPS
    for r in range(tm // CHUNK):
        rs = slice(r * CHUNK, (r + 1) * CHUNK)
        for g in range(N_GROUPS):
            gs = slice(g * gd, (g + 1) * gd)
            z = _dot(wmix_ref[0, g], vb_scr[rs, gs]) + bias_ref[0, :, gs]
            y_scr[rs, gs] = (u_scr[rs, gs] * z).astype(BF16)
    m = _dot(y_scr[...], wout_ref[...])
    out_ref[...] = x + _rms(m, g_ref[1:2, :])


def _gmlp_layer(x, n_prompt, g4, w_in, ln_g, ln_b, wmix, bias, w_out):
    n, d = x.shape
    d_inner = w_out.shape[0]
    tm = _tile(math.gcd(n_prompt, n - n_prompt), 2 * CHUNK)
    assert tm % CHUNK == 0 and d_inner // N_GROUPS == LANES
    n_pt = n_prompt // tm
    row = lambda i: (i, 0)
    sel = lambda i: (jnp.where(i < n_pt, 0, 1), 0, 0)
    return pl.pallas_call(
        functools.partial(_gmlp_kernel, col_w=512),
        grid=(n // tm,),
        in_specs=[
            pl.BlockSpec((tm, d), row),
            _resident((4, d)),
            _resident(w_in.shape),
            _resident((1, d_inner)),
            _resident((1, d_inner)),
            pl.BlockSpec((1, N_GROUPS, CHUNK, CHUNK), lambda i: (jnp.where(i < n_pt, 0, 1), 0, 0, 0)),
            pl.BlockSpec((1, CHUNK, d_inner), sel),
            _resident(w_out.shape),
        ],
        out_specs=[pl.BlockSpec((tm, d), row), pl.BlockSpec((tm, d_inner), row)],
        out_shape=[jax.ShapeDtypeStruct((n, d), F32), jax.ShapeDtypeStruct((n, d_inner), F32)],
        scratch_shapes=[
            pltpu.VMEM((tm, d), BF16),
            pltpu.VMEM((tm, d_inner), F32),
            pltpu.VMEM((tm, d_inner), BF16),
            pltpu.VMEM((tm, d_inner), BF16),
        ],
        compiler_params=_params(("parallel",)),
        name="gmlp_layer",
    )(x, g4, w_in, ln_g, ln_b, wmix, bias, w_out)


def _swiglu_step(h, wg_ref, wu_ref, wd_ref, acc_ref):
    g = _dot(h, wg_ref[...])
    u = _dot(h, wu_ref[...])
    a = (jax.nn.silu(g) * u).astype(BF16)
    acc_ref[...] += _dot(a, wd_ref[...])


def _swiglu_kernel(x_ref, g_ref, wg_ref, wu_ref, wd_ref, out_ref, h_scr, acc_scr):
    f = pl.program_id(1)

    @pl.when(f == 0)
    def _():
        h_scr[...] = _rms(x_ref[...], g_ref[2:3, :]).astype(BF16)
        acc_scr[...] = jnp.zeros_like(acc_scr)

    _swiglu_step(h_scr[...], wg_ref, wu_ref, wd_ref, acc_scr)

    @pl.when(f == pl.num_programs(1) - 1)
    def _():
        out_ref[...] = x_ref[...] + _rms(acc_scr[...], g_ref[3:4, :])


def _swiglu_layer(x, g4, w_gu, w_down, tm_pref=512, tf_pref=512):
    n, d = x.shape
    d_ff = w_down.shape[0]
    tm = _tile(n, tm_pref)
    tf = _tile(d_ff, tf_pref)
    nf = d_ff // tf
    return pl.pallas_call(
        _swiglu_kernel,
        grid=(n // tm, nf),
        in_specs=[
            pl.BlockSpec((tm, d), lambda i, f: (i, 0)),
            pl.BlockSpec((4, d), lambda i, f: (0, 0)),
            pl.BlockSpec((d, tf), lambda i, f: (0, f)),
            pl.BlockSpec((d, tf), lambda i, f: (0, f + nf)),
            pl.BlockSpec((tf, d), lambda i, f: (f, 0)),
        ],
        out_specs=pl.BlockSpec((tm, d), lambda i, f: (i, 0)),
        out_shape=jax.ShapeDtypeStruct((n, d), F32),
        scratch_shapes=[pltpu.VMEM((tm, d), BF16), pltpu.VMEM((tm, d), F32)],
        compiler_params=_params(("parallel", "arbitrary")),
        name="swiglu_dense",
    )(x, g4, w_gu, w_gu, w_down)


def _experts_kernel(te_ref, na_ref, xs_ref, wg_ref, wu_ref, wd_ref, ys_ref, h_scr, acc_scr):
    i = pl.program_id(0)
    f = pl.program_id(1)
    last = pl.num_programs(1) - 1
    active = i < na_ref[0]

    @pl.when(active & (f == 0))
    def _():
        h_scr[...] = xs_ref[...].astype(BF16)
        acc_scr[...] = jnp.zeros_like(acc_scr)

    @pl.when(active)
    def _():
        _swiglu_step(h_scr[...], wg_ref.at[0], wu_ref.at[0], wd_ref.at[0], acc_scr)

    @pl.when(active & (f == last))
    def _():
        ys_ref[...] = acc_scr[...]

    @pl.when(jnp.logical_not(active) & (f == last))
    def _():
        ys_ref[...] = jnp.zeros_like(ys_ref)


def _experts(xs, tile_expert, n_active, w_gu, w_down, tm, tf_pref=512):
    rows, d = xs.shape
    d_ff = w_down.shape[1]
    tf = _tile(d_ff, tf_pref)
    nf = d_ff // tf

    def fidx(i, f, na):
        return jnp.where(i < na[0], f, nf - 1)

    grid_spec = pltpu.PrefetchScalarGridSpec(
        num_scalar_prefetch=2,
        grid=(rows // tm, nf),
        in_specs=[
            pl.BlockSpec((tm, d), lambda i, f, te, na: (i, 0)),
            pl.BlockSpec((1, d, tf), lambda i, f, te, na: (te[i], 0, fidx(i, f, na))),
            pl.BlockSpec((1, d, tf), lambda i, f, te, na: (te[i], 0, fidx(i, f, na) + nf)),
            pl.BlockSpec((1, tf, d), lambda i, f, te, na: (te[i], fidx(i, f, na), 0)),
        ],
        out_specs=pl.BlockSpec((tm, d), lambda i, f, te, na: (i, 0)),
        scratch_shapes=[pltpu.VMEM((tm, d), BF16), pltpu.VMEM((tm, d), F32)],
    )
    return pl.pallas_call(
        _experts_kernel,
        grid_spec=grid_spec,
        out_shape=jax.ShapeDtypeStruct((rows, d), F32),
        compiler_params=_params(("arbitrary", "arbitrary")),
        name="swiglu_experts",
    )(tile_expert, n_active, xs, w_gu, w_gu, w_down)


def _rope(x, cos, sin_signed):
    parts = []
    for s in range(x.shape[1] // HEAD_DIM):
        xs = x[:, s * HEAD_DIM:(s + 1) * HEAD_DIM]
        parts.append(xs * cos + pltpu.roll(xs, HEAD_DIM // 2, axis=1) * sin_signed)
    return jnp.concatenate(parts, axis=1)


def _qkv_kernel(x_ref, g_ref, cos_ref, sin_ref, wq_ref, wk_ref, wv_ref, q_ref, k_ref, v_ref, h_scr):
    @pl.when(pl.program_id(1) == 0)
    def _():
        h_scr[...] = _rms(x_ref[...], g_ref[0:1, :]).astype(BF16)

    h = h_scr[...]
    cos = cos_ref[...]
    sin = sin_ref[...]
    q_ref[...] = _rope(_dot(h, wq_ref[...]), cos, sin).astype(BF16)
    k_ref[...] = _rope(_dot(h, wk_ref[...]), cos, sin)
    v_ref[...] = _dot(h, wv_ref[...])


def _qkv_layer(x, n_prompt, seq, g4, cos_tab, sin_tab, w_qkv, tm, tn_pref=512):
    n, d = x.shape
    width = w_qkv.shape[1] // 3
    tn = _tile(width, tn_pref)
    nj = width // tn
    n_pt = n_prompt // tm
    per_seq = seq // tm
    tab = lambda i, j: (jnp.where(i < n_pt, i % per_seq, per_seq), 0)
    blk = lambda i, j: (i, j)
    return pl.pallas_call(
        _qkv_kernel,
        grid=(n // tm, nj),
        in_specs=[
            pl.BlockSpec((tm, d), lambda i, j: (i, 0)),
            pl.BlockSpec((4, d), lambda i, j: (0, 0)),
            pl.BlockSpec((tm, HEAD_DIM), tab),
            pl.BlockSpec((tm, HEAD_DIM), tab),
            pl.BlockSpec((d, tn), lambda i, j: (0, j)),
            pl.BlockSpec((d, tn), lambda i, j: (0, j + nj)),
            pl.BlockSpec((d, tn), lambda i, j: (0, j + 2 * nj)),
        ],
        out_specs=[pl.BlockSpec((tm, tn), blk)] * 3,
        out_shape=[
            jax.ShapeDtypeStruct((n, width), BF16),
            jax.ShapeDtypeStruct((n, width), F32),
            jax.ShapeDtypeStruct((n, width), F32),
        ],
        scratch_shapes=[pltpu.VMEM((tm, d), BF16)],
        compiler_params=_params(("parallel", "arbitrary")),
        name="qkv_rope",
    )(x, g4, cos_tab, sin_tab, w_qkv, w_qkv, w_qkv)


def _lambda(lam_ref, lam_init):
    lp = lam_ref[...]
    a = jnp.sum(lp[0:1, :] * lp[1:2, :], axis=-1, keepdims=True)
    b = jnp.sum(lp[2:3, :] * lp[3:4, :], axis=-1, keepdims=True)
    return jnp.exp(a) - jnp.exp(b) + lam_init


def _softmax_update(s, v_bf16, m_ref, l_ref, acc_ref):
    m_prev = m_ref[...]
    m_new = jnp.maximum(m_prev, jnp.max(s, axis=-1, keepdims=True))
    alpha = jnp.exp(m_prev - m_new)
    p = jnp.exp(s - m_new)
    l_ref[...] = alpha * l_ref[...] + jnp.sum(p, axis=-1, keepdims=True)
    acc_ref[...] = alpha * acc_ref[...] + _dot(p.astype(BF16), v_bf16)
    m_ref[...] = m_new


def _prompt_attn_kernel(lam_ref, sub_ref, q_ref, k_ref, v_ref, o_ref,
                        k_scr, v_scr, m_scr, l_scr, acc_scr, *, lam_init, scale):
    qi = pl.program_id(2)
    tq = q_ref.shape[0]

    @pl.when(qi == 0)
    def _():
        k_scr[...] = k_ref[...].astype(BF16)
        v_scr[...] = v_ref[...].astype(BF16)

    m_scr[...] = jnp.full_like(m_scr, NEG)
    l_scr[...] = jnp.zeros_like(l_scr)
    acc_scr[...] = jnp.zeros_like(acc_scr)

    def block(j, masked):
        rows = pl.ds(pl.multiple_of(j * tq, tq), tq)
        vj = v_scr[rows, :]
        for c in range(2):
            cs = slice(c * HEAD_DIM, (c + 1) * HEAD_DIM)
            s = _dot_nt(q_ref[:, cs], k_scr[rows, cs]) * scale
            if masked:
                r_id = lax.broadcasted_iota(I32, s.shape, 0)
                c_id = lax.broadcasted_iota(I32, s.shape, 1)
                s = jnp.where(c_id <= r_id, s, NEG)
            _softmax_update(s, vj, m_scr.at[c], l_scr.at[c], acc_scr.at[c])

    def body(j, carry):
        block(j, False)
        return carry

    lax.fori_loop(0, qi, body, 0)
    block(qi, True)

    lam = _lambda(lam_ref, lam_init)
    o = acc_scr[0] / l_scr[0] - lam * (acc_scr[1] / l_scr[1])
    o_ref[...] = (_rms(o, sub_ref[...]) * (1.0 - lam_init)).astype(BF16)


def _prompt_attn(q, k, v, b_lam, subln, batch, seq, lam_init, tq_pref=512):
    n_heads = q.shape[1] // (2 * HEAD_DIM)
    hw = 2 * HEAD_DIM
    tq = _tile(seq, tq_pref)
    nq = seq // tq
    kv_spec = pl.BlockSpec((seq, hw), lambda b, h, i: (b, h))
    return pl.pallas_call(
        functools.partial(_prompt_attn_kernel, lam_init=lam_init, scale=HEAD_DIM ** -0.5),
        grid=(batch, n_heads, nq),
        in_specs=[
            pl.BlockSpec((4, HEAD_DIM), lambda b, h, i: (0, 0)),
            pl.BlockSpec((1, hw), lambda b, h, i: (0, 0)),
            pl.BlockSpec((tq, hw), lambda b, h, i: (b * nq + i, h)),
            kv_spec,
            kv_spec,
        ],
        out_specs=pl.BlockSpec((tq, hw), lambda b, h, i: (b * nq + i, h)),
        out_shape=jax.ShapeDtypeStruct((batch * seq, n_heads * hw), BF16),
        scratch_shapes=[
            pltpu.VMEM((seq, hw), BF16),
            pltpu.VMEM((seq, hw), BF16),
            pltpu.VMEM((2, tq, 1), F32),
            pltpu.VMEM((2, tq, 1), F32),
            pltpu.VMEM((2, tq, hw), F32),
        ],
        compiler_params=_params(("parallel", "parallel", "arbitrary")),
        name="diff_attn_prompt",
    )(b_lam, subln, q, k, v)


def _decode_attn_kernel(pt_ref, lam_ref, sub_ref, q_ref, kn_ref, vn_ref, *rest,
                        pages_per_step, dec_seq, lam_init, scale):
    k_refs = rest[:pages_per_step]
    v_refs = rest[pages_per_step:2 * pages_per_step]
    o_ref, m_scr, l_scr, acc_scr = rest[2 * pages_per_step:]
    del pt_ref
    step = pl.program_id(1)
    n_heads = acc_scr.shape[0]
    hw = 2 * HEAD_DIM
    rows = 2 * SUBLANES

    @pl.when(step == 0)
    def _():
        m_scr[...] = jnp.full_like(m_scr, NEG)
        l_scr[...] = jnp.zeros_like(l_scr)
        acc_scr[...] = jnp.zeros_like(acc_scr)

    for k_ref, v_ref in zip(k_refs, v_refs):
        for h in range(n_heads):
            s = jnp.concatenate(
                [_dot_nt(q_ref[0, :, (2 * h + c) * HEAD_DIM:(2 * h + c + 1) * HEAD_DIM],
                         k_ref[0, :, (2 * h + c) * HEAD_DIM:(2 * h + c + 1) * HEAD_DIM].astype(BF16))
                 for c in range(2)], axis=0) * scale
            vh = v_ref[0, :, h * hw:(h + 1) * hw].astype(BF16)
            _softmax_update(s, vh, m_scr.at[h], l_scr.at[h], acc_scr.at[h])

    @pl.when(step == pl.num_programs(1) - 1)
    def _():
        lam = _lambda(lam_ref, lam_init)
        lane = lax.broadcasted_iota(I32, (rows, LANES), 1)
        q_id = lax.broadcasted_iota(I32, (rows, LANES), 0) % SUBLANES
        for h in range(n_heads):
            s = jnp.full((rows, LANES), NEG, F32)
            for t in range(dec_seq):
                st = jnp.concatenate(
                    [jnp.sum(q_ref[0, :, (2 * h + c) * HEAD_DIM:(2 * h + c + 1) * HEAD_DIM].astype(F32)
                             * kn_ref[0, t:t + 1, (2 * h + c) * HEAD_DIM:(2 * h + c + 1) * HEAD_DIM],
                             axis=-1, keepdims=True) for c in range(2)], axis=0) * scale
                s = jnp.where((lane == t) & (t <= q_id), st, s)
            m_prev = m_scr[h]
            m_new = jnp.maximum(m_prev, jnp.max(s, axis=-1, keepdims=True))
            alpha = jnp.exp(m_prev - m_new)
            p = jnp.exp(s - m_new)
            l_fin = alpha * l_scr[h] + jnp.sum(p, axis=-1, keepdims=True)
            acc = alpha * acc_scr[h]
            for t in range(dec_seq):
                pt = jnp.sum(jnp.where(lane == t, p, 0.0), axis=-1, keepdims=True)
                acc = acc + pt * vn_ref[0, t:t + 1, h * hw:(h + 1) * hw]
            o = acc / l_fin
            o = o[:SUBLANES] - lam * o[SUBLANES:]
            o_ref[0, :, h * hw:(h + 1) * hw] = (_rms(o, sub_ref[...]) * (1.0 - lam_init)).astype(BF16)


def _decode_attn(q8, kn8, vn8, cache_k, cache_v, page_table, b_lam, subln, dec_seq, lam_init,
                 pages_per_step=2):
    dec_batch, _, width = q8.shape
    n_pages = page_table.shape[1]
    page = cache_k.shape[1]
    n_heads = width // (2 * HEAD_DIM)
    pps = pages_per_step if n_pages % pages_per_step == 0 else 1
    n_steps = n_pages // pps

    def page_spec(u):
        return pl.BlockSpec((1, page, width), lambda s, p, pt: (pt[s * n_pages + p * pps + u], 0, 0))

    seq_spec = pl.BlockSpec((1, SUBLANES, width), lambda s, p, pt: (s, 0, 0))
    grid_spec = pltpu.PrefetchScalarGridSpec(
        num_scalar_prefetch=1,
        grid=(dec_batch, n_steps),
        in_specs=[
            pl.BlockSpec((4, HEAD_DIM), lambda s, p, pt: (0, 0)),
            pl.BlockSpec((1, 2 * HEAD_DIM), lambda s, p, pt: (0, 0)),
            seq_spec, seq_spec, seq_spec,
        ] + [page_spec(u) for u in range(pps)] * 2,
        out_specs=seq_spec,
        scratch_shapes=[
            pltpu.VMEM((n_heads, 2 * SUBLANES, 1), F32),
            pltpu.VMEM((n_heads, 2 * SUBLANES, 1), F32),
            pltpu.VMEM((n_heads, 2 * SUBLANES, 2 * HEAD_DIM), F32),
        ],
    )
    return pl.pallas_call(
        functools.partial(_decode_attn_kernel, pages_per_step=pps, dec_seq=dec_seq,
                          lam_init=lam_init, scale=HEAD_DIM ** -0.5),
        grid_spec=grid_spec,
        out_shape=jax.ShapeDtypeStruct((dec_batch, SUBLANES, width), BF16),
        compiler_params=_params(("parallel", "arbitrary")),
        name="diff_attn_decode",
    )(page_table.reshape(-1), b_lam, subln, q8, kn8, vn8,
      *([cache_k] * pps), *([cache_v] * pps))


def _wo_router_kernel(o_ref, x_ref, g_ref, wo_ref, wr_ref, br_ref, tri_ref,
                      x3_ref, hm_ref, meta_ref, cnt_ref, carry_scr, *, n_experts):
    i = pl.program_id(0)

    @pl.when(i == 0)
    def _():
        carry_scr[...] = jnp.zeros_like(carry_scr)

    x3 = x_ref[...] + _rms(_dot(o_ref[...], wo_ref[...]), g_ref[1:2, :])
    x3_ref[...] = x3
    hm = _rms(x3, g_ref[2:3, :])
    hm_ref[...] = hm
    logits = jnp.dot(hm, wr_ref[...], preferred_element_type=F32,
                     precision=lax.Precision.HIGHEST) + br_ref[...]
    lane = lax.broadcasted_iota(I32, logits.shape, 1)
    logits = jnp.where(lane < n_experts, logits, -jnp.inf)
    m1 = jnp.max(logits, axis=-1, keepdims=True)
    i1 = jnp.min(jnp.where(logits == m1, lane, LANES), axis=-1, keepdims=True)
    rest = jnp.where(lane == i1, -jnp.inf, logits)
    m2 = jnp.max(rest, axis=-1, keepdims=True)
    i2 = jnp.min(jnp.where(rest == m2, lane, LANES), axis=-1, keepdims=True)
    e2 = jnp.exp(m2 - m1)
    g1 = 1.0 / (1.0 + e2)
    g2 = e2 / (1.0 + e2)
    oh1 = (lane == i1).astype(F32)
    oh2 = (lane == i2).astype(F32)
    tri = tri_ref[...]
    before1 = _dot(tri, oh1.astype(BF16))
    before2 = _dot(tri, oh2.astype(BF16))
    tot1 = jnp.sum(oh1, axis=0, keepdims=True)
    tot2 = jnp.sum(oh2, axis=0, keepdims=True)
    carry = carry_scr[...]
    r1 = jnp.sum(oh1 * (carry + before1), axis=-1, keepdims=True)
    r2 = jnp.sum(oh2 * (carry + tot1 + before2), axis=-1, keepdims=True)
    carry = carry + tot1 + tot2
    carry_scr[...] = carry
    cnt_ref[...] = carry
    meta = jnp.zeros(logits.shape, F32)
    for col, val in enumerate((i1.astype(F32), i2.astype(F32), g1, g2, r1, r2)):
        meta = jnp.where(lane == col, val, meta)
    meta_ref[...] = meta


def _wo_router(o, x, g4, w_o, w_r, b_r, tm, n_experts):
    n, d = x.shape
    row = lambda i: (i, 0)
    tri = jnp.tril(jnp.ones((tm, tm), BF16), k=-1)
    return pl.pallas_call(
        functools.partial(_wo_router_kernel, n_experts=n_experts),
        grid=(n // tm,),
        in_specs=[
            pl.BlockSpec((tm, d), row),
            pl.BlockSpec((tm, d), row),
            _resident((4, d)),
            _resident(w_o.shape),
            _resident(w_r.shape),
            _resident(b_r.shape),
            _resident((tm, tm)),
        ],
        out_specs=[
            pl.BlockSpec((tm, d), row),
            pl.BlockSpec((tm, d), row),
            pl.BlockSpec((tm, LANES), row),
            pl.BlockSpec((1, LANES), lambda i: (0, 0)),
        ],
        out_shape=[
            jax.ShapeDtypeStruct((n, d), F32),
            jax.ShapeDtypeStruct((n, d), F32),
            jax.ShapeDtypeStruct((n, LANES), F32),
            jax.ShapeDtypeStruct((1, LANES), F32),
        ],
        scratch_shapes=[pltpu.VMEM((1, LANES), F32)],
        compiler_params=_params(("arbitrary",)),
        name="wo_router",
    )(o, x, g4, w_o, w_r, b_r, tri)


def _row_copy(src_ref, src_row, dst_ref, dst_row, sem):
    return pltpu.make_async_copy(src_ref.at[pl.ds(src_row, 1), :], dst_ref.at[pl.ds(dst_row, 1), :], sem)


def _dispatch_kernel(pos_ref, hm_ref, init_ref, xs_ref, sem):
    del init_ref
    tm = hm_ref.shape[0]
    base = pl.program_id(0) * tm * TOP_K

    def start(r, carry):
        for k in range(TOP_K):
            _row_copy(hm_ref, r, xs_ref, pos_ref[base + r * TOP_K + k], sem).start()
        return carry

    def wait(r, carry):
        for k in range(TOP_K):
            _row_copy(hm_ref, r, xs_ref, pos_ref[base + r * TOP_K + k], sem).wait()
        return carry

    lax.fori_loop(0, tm, start, 0)
    lax.fori_loop(0, tm, wait, 0)


def _dispatch(hm, pos_flat, rows, tm):
    n, d = hm.shape
    grid_spec = pltpu.PrefetchScalarGridSpec(
        num_scalar_prefetch=1,
        grid=(n // tm,),
        in_specs=[
            pl.BlockSpec((tm, d), lambda i, pos: (i, 0)),
            pl.BlockSpec(memory_space=pl.ANY),
        ],
        out_specs=pl.BlockSpec(memory_space=pl.ANY),
        scratch_shapes=[pltpu.SemaphoreType.DMA(())],
    )
    return pl.pallas_call(
        _dispatch_kernel,
        grid_spec=grid_spec,
        out_shape=jax.ShapeDtypeStruct((rows, d), F32),
        input_output_aliases={2: 0},
        compiler_params=_params(("arbitrary",)),
        name="moe_dispatch",
    )(pos_flat, hm, jnp.zeros((rows, d), F32))


def _combine_kernel(pos_ref, x_ref, g_ref, meta_ref, ys_ref, out_ref, a_scr, b_scr, sem):
    tm = x_ref.shape[0]
    base = pl.program_id(0) * tm * TOP_K
    bufs = (a_scr, b_scr)

    def start(r, carry):
        for k in range(TOP_K):
            _row_copy(ys_ref, pos_ref[base + r * TOP_K + k], bufs[k], r, sem).start()
        return carry

    def wait(r, carry):
        for k in range(TOP_K):
            _row_copy(ys_ref, pos_ref[base + r * TOP_K + k], bufs[k], r, sem).wait()
        return carry

    lax.fori_loop(0, tm, start, 0)
    lax.fori_loop(0, tm, wait, 0)
    meta = meta_ref[...]
    f = meta[:, 2:3] * a_scr[...] + meta[:, 3:4] * b_scr[...]
    out_ref[...] = x_ref[...] + _rms(f, g_ref[3:4, :])


def _combine(x, g4, meta, ys, pos_flat, tm):
    n, d = x.shape
    grid_spec = pltpu.PrefetchScalarGridSpec(
        num_scalar_prefetch=1,
        grid=(n // tm,),
        in_specs=[
            pl.BlockSpec((tm, d), lambda i, pos: (i, 0)),
            pl.BlockSpec((4, d), lambda i, pos: (0, 0)),
            pl.BlockSpec((tm, LANES), lambda i, pos: (i, 0)),
            pl.BlockSpec(memory_space=pl.ANY),
        ],
        out_specs=pl.BlockSpec((tm, d), lambda i, pos: (i, 0)),
        scratch_shapes=[pltpu.VMEM((tm, d), F32), pltpu.VMEM((tm, d), F32),
                        pltpu.SemaphoreType.DMA(())],
    )
    return pl.pallas_call(
        _combine_kernel,
        grid_spec=grid_spec,
        out_shape=jax.ShapeDtypeStruct((n, d), F32),
        compiler_params=_params(("arbitrary",)),
        name="moe_combine",
    )(pos_flat, x, g4, meta, ys)


def _rope_tables(seq, past, dec_seq, tm):
    inv = jnp.exp(-math.log(ROPE_THETA) * jnp.arange(0, HEAD_DIM, 2, dtype=F32) / HEAD_DIM)
    pos = jnp.concatenate([jnp.arange(seq), past + jnp.arange(tm) % dec_seq])
    ang = pos.astype(F32)[:, None] * inv[None, :]
    ang = jnp.concatenate([ang, ang], axis=-1)
    sign = jnp.where(jnp.arange(HEAD_DIM) < HEAD_DIM // 2, -1.0, 1.0).astype(F32)
    return jnp.cos(ang), jnp.sin(ang) * sign


def _mix_tables(w_s, b_s, dec_seq):
    causal = jnp.tril(jnp.ones((CHUNK, CHUNK), dtype=bool))
    w_prompt = jnp.where(causal[None], w_s, 0)
    per = CHUNK // dec_seq
    w_small = w_prompt[:, :dec_seq, :dec_seq]
    eye = jnp.eye(per, dtype=w_s.dtype)
    w_sample = jnp.einsum("ab,gts->gatbs", eye, w_small).reshape(N_GROUPS, CHUNK, CHUNK)
    wmix = jnp.stack([w_prompt, w_sample]).astype(BF16)
    gd = LANES
    b_prompt = jnp.repeat(jnp.transpose(b_s), gd, axis=1)
    b_sample = jnp.repeat(jnp.tile(jnp.transpose(b_s)[:dec_seq], (per, 1)), gd, axis=1)
    return wmix, jnp.stack([b_prompt, b_sample])


def kernel(x_prompt, x_sample, cache_k, cache_v, page_table, norm_g, a_w_in, a_ln_g, a_ln_b, a_w_s,
           a_b_s, a_w_out, f_w_gu, f_w_down, b_w_qkv, b_lam, b_subln, b_w_o, e_w_r, e_b_r, e_w_gu,
           e_w_down):
    batch, seq, d = x_prompt.shape
    dec_batch, dec_seq, _ = x_sample.shape
    assert norm_g.shape[0] == 2 and a_w_in.shape[0] == 1 and b_w_qkv.shape[0] == 1
    n_p, n_s = batch * seq, dec_batch * dec_seq
    n = n_p + n_s
    n_experts = e_w_r.shape[-1]
    n_heads = d // (2 * HEAD_DIM)
    page = cache_k.shape[2]
    past = page_table.shape[1] * page
    assert CHUNK % dec_seq == 0 and dec_seq <= SUBLANES
    tm = _tile(math.gcd(n_p, n_s), 512)
    assert seq % tm == 0 and tm % dec_seq == 0

    x = jnp.concatenate([x_prompt.reshape(n_p, d), x_sample.reshape(n_s, d)], axis=0)

    wmix, bias = _mix_tables(a_w_s[0], a_b_s[0], dec_seq)
    x, v_rows = _gmlp_layer(x, n_p, norm_g[0], a_w_in[0].astype(BF16), a_ln_g, a_ln_b, wmix, bias,
                            a_w_out[0].astype(BF16))
    x = _swiglu_layer(x, norm_g[0], f_w_gu[0].astype(BF16), f_w_down[0].astype(BF16), tm_pref=tm)

    lam_init = 0.8 - 0.6 * math.exp(-0.3 * 1)
    cos_tab, sin_tab = _rope_tables(seq, past, dec_seq, tm)
    q, k, v = _qkv_layer(x, n_p, seq, norm_g[1], cos_tab, sin_tab, b_w_qkv[0].astype(BF16), tm)
    o_p = _prompt_attn(q, k, v, b_lam[0], b_subln, batch, seq, lam_init)
    pad8 = lambda a: jnp.pad(a[n_p:].reshape(dec_batch, dec_seq, d),
                             ((0, 0), (0, SUBLANES - dec_seq), (0, 0)))
    o_s = _decode_attn(pad8(q), pad8(k), pad8(v),
                       cache_k[0].reshape(-1, page, d), cache_v[0].reshape(-1, page, d),
                       page_table, b_lam[0], b_subln, dec_seq, lam_init)
    o = jnp.concatenate([o_p, o_s[:, :dec_seq].reshape(n_s, d)], axis=0)

    w_r = jnp.pad(e_w_r[0], ((0, 0), (0, LANES - n_experts)))
    b_r = jnp.pad(e_b_r, ((0, 0), (0, LANES - n_experts)))
    x, hm, meta, cnt = _wo_router(o, x, norm_g[1], b_w_o[0].astype(BF16), w_r, b_r, tm, n_experts)
    expert = meta[:, 0:TOP_K].astype(I32)
    rank = meta[:, 4:4 + TOP_K].astype(I32)
    counts = cnt[0, :n_experts].astype(I32)
    tiles_per = (counts + tm - 1) // tm
    tile_end = jnp.cumsum(tiles_per)
    starts = (tile_end - tiles_per) * tm
    pos_flat = (starts[expert] + rank).reshape(-1)
    n_tiles = (TOP_K * n + n_experts * (tm - 1)) // tm
    n_active = tile_end[-1:]
    tile_id = jnp.minimum(jnp.arange(n_tiles), n_active[0] - 1)
    tile_expert = jnp.sum(tile_id[:, None] >= tile_end[None, :], axis=1).astype(I32)
    xs = _dispatch(hm, pos_flat, n_tiles * tm, tm)
    ys = _experts(xs, tile_expert, n_active.astype(I32), e_w_gu[0].astype(BF16),
                  e_w_down[0].astype(BF16), tm)
    x = _combine(x, norm_g[1], meta, ys, pos_flat, tm)

    kv_shape = lambda b, t, *tail: (1, b, t) + tail
    return (
        x[:n_p].reshape(batch, seq, d),
        x[n_p:].reshape(dec_batch, dec_seq, d),
        k[:n_p].reshape(kv_shape(batch, seq, n_heads, 2, HEAD_DIM)),
        v[:n_p].reshape(kv_shape(batch, seq, n_heads, 2 * HEAD_DIM)),
        k[n_p:].reshape(kv_shape(dec_batch, dec_seq, n_heads, 2, HEAD_DIM)),
        v[n_p:].reshape(kv_shape(dec_batch, dec_seq, n_heads, 2 * HEAD_DIM)),
        v_rows[n_p:].reshape(1, dec_batch, dec_seq, -1),
    )
```

```python
import functools
import math

import jax
import jax.numpy as jnp
from jax import lax
from jax.experimental import pallas as pl
from jax.experimental.pallas import tpu as pltpu

F32 = jnp.float32
BF16 = jnp.bfloat16
I32 = jnp.int32

EPS = 1e-6
CHUNK = 128
N_GROUPS = 16
HEAD_DIM = 128
ROPE_THETA = 10000.0
TOP_K = 2
NEG = -1e30
LANES = 128
SUBLANES = 8
VMEM_LIMIT = 56 * 1024 * 1024


def _dot(a, b):
    return jnp.dot(a, b, preferred_element_type=F32)


def _dot_nt(a, b):
    return lax.dot_general(a, b, (((1,), (1,)), ((), ())), preferred_element_type=F32)


def _rms(x, g):
    return x * lax.rsqrt(jnp.mean(x * x, axis=-1, keepdims=True) + EPS) * g


def _tile(n, pref):
    t = min(n, pref)
    while n % t:
        t -= SUBLANES
    assert t > 0 and t % SUBLANES == 0, (n, pref)
    return t


def _params(sem, vmem=VMEM_LIMIT):
    return pltpu.CompilerParams(dimension_semantics=sem, vmem_limit_bytes=vmem)


def _resident(shape):
    return pl.BlockSpec(shape, lambda *_: (0,) * len(shape), pipeline_mode=pl.Buffered(1))


def _gmlp_kernel(x_ref, g_ref, win_ref, lng_ref, lnb_ref, wmix_ref, bias_ref, wout_ref,
                 out_ref, v_ref, h_scr, u_scr, vb_scr, y_scr, *, col_w):
    tm, d = x_ref.shape
    d_inner = u_scr.shape[1]
    x = x_ref[...]
    h_scr[...] = _rms(x, g_ref[0:1, :]).astype(BF16)
    n_col = d_inner // col_w
    s1 = jnp.zeros((tm, 1), F32)
    for c in range(n_col):
        cs = slice(c * col_w, (c + 1) * col_w)
        u_scr[:, cs] = jax.nn.gelu(_dot(h_scr[...], win_ref[:, cs]))
        v = jax.nn.gelu(_dot(h_scr[...], win_ref[:, d_inner + c * col_w:d_inner + (c + 1) * col_w]))
        v_ref[:, cs] = v
        s1 = s1 + jnp.sum(v, axis=-1, keepdims=True)
    mu = s1 / d_inner
    s2 = jnp.zeros((tm, 1), F32)
    for c in range(n_col):
        cs = slice(c * col_w, (c + 1) * col_w)
        dv = v_ref[:, cs] - mu
        s2 = s2 + jnp.sum(dv * dv, axis=-1, keepdims=True)
    rstd = lax.rsqrt(s2 / d_inner + EPS)
    for c in range(n_col):
        cs = slice(c * col_w, (c + 1) * col_w)
        vn = (v_ref[:, cs] - mu) * rstd * lng_ref[:, cs] + lnb_ref[:, cs]
        v_ref[:, cs] = vn
        vb_scr[:, cs] = vn.astype(BF16)
    gd = d_inner // N_GROUPS
    for r in range(tm // CHUNK):
        rs = slice(r * CHUNK, (r + 1) * CHUNK)
        for g in range(N_GROUPS):
            gs = slice(g * gd, (g + 1) * gd)
            z = _dot(wmix_ref[g], vb_scr[rs, gs]) + bias_ref[:, gs]
            y_scr[rs, gs] = (u_scr[rs, gs] * z).astype(BF16)
    m = _dot(y_scr[...], wout_ref[...])
    out_ref[...] = x + _rms(m, g_ref[1:2, :])


def _gmlp_layer(x, g4, w_in, ln_g, ln_b, wmix, bias, w_out):
    n, d = x.shape
    d_inner = w_out.shape[0]
    tm = _tile(n, 2 * CHUNK)
    assert tm % CHUNK == 0 and d_inner // N_GROUPS == LANES
    row = lambda i: (i, 0)
    return pl.pallas_call(
        functools.partial(_gmlp_kernel, col_w=512),
        grid=(n // tm,),
        in_specs=[
            pl.BlockSpec((tm, d), row),
            _resident((4, d)),
            _resident(w_in.shape),
            _resident((1, d_inner)),
            _resident((1, d_inner)),
            _resident(wmix.shape),
            _resident(bias.shape),
            _resident(w_out.shape),
        ],
        out_specs=[pl.BlockSpec((tm, d), row), pl.BlockSpec((tm, d_inner), row)],
        out_shape=[jax.ShapeDtypeStruct((n, d), F32), jax.ShapeDtypeStruct((n, d_inner), F32)],
        scratch_shapes=[
            pltpu.VMEM((tm, d), BF16),
            pltpu.VMEM((tm, d_inner), F32),
            pltpu.VMEM((tm, d_inner), BF16),
            pltpu.VMEM((tm, d_inner), BF16),
        ],
        compiler_params=_params(("parallel",)),
        name="gmlp_layer",
    )(x, g4, w_in, ln_g, ln_b, wmix, bias, w_out)


def _swiglu_step(h, wg_ref, wu_ref, wd_ref, acc_ref):
    g = _dot(h, wg_ref[...])
    u = _dot(h, wu_ref[...])
    a = (jax.nn.silu(g) * u).astype(BF16)
    acc_ref[...] += _dot(a, wd_ref[...])


def _swiglu_kernel(x_ref, g_ref, wg_ref, wu_ref, wd_ref, out_ref, h_scr, acc_scr):
    f = pl.program_id(1)

    @pl.when(f == 0)
    def _():
        h_scr[...] = _rms(x_ref[...], g_ref[2:3, :]).astype(BF16)
        acc_scr[...] = jnp.zeros_like(acc_scr)

    _swiglu_step(h_scr[...], wg_ref, wu_ref, wd_ref, acc_scr)

    @pl.when(f == pl.num_programs(1) - 1)
    def _():
        out_ref[...] = x_ref[...] + _rms(acc_scr[...], g_ref[3:4, :])


def _swiglu_layer(x, g4, w_gu, w_down, tm_pref=512, tf_pref=512):
    n, d = x.shape
    d_ff = w_down.shape[0]
    tm = _tile(n, tm_pref)
    tf = _tile(d_ff, tf_pref)
    nf = d_ff // tf
    return pl.pallas_call(
        _swiglu_kernel,
        grid=(n // tm, nf),
        in_specs=[
            pl.BlockSpec((tm, d), lambda i, f: (i, 0)),
            pl.BlockSpec((4, d), lambda i, f: (0, 0)),
            pl.BlockSpec((d, tf), lambda i, f: (0, f)),
            pl.BlockSpec((d, tf), lambda i, f: (0, f + nf)),
            pl.BlockSpec((tf, d), lambda i, f: (f, 0)),
        ],
        out_specs=pl.BlockSpec((tm, d), lambda i, f: (i, 0)),
        out_shape=jax.ShapeDtypeStruct((n, d), F32),
        scratch_shapes=[pltpu.VMEM((tm, d), BF16), pltpu.VMEM((tm, d), F32)],
        compiler_params=_params(("parallel", "arbitrary")),
        name="swiglu_dense",
    )(x, g4, w_gu, w_gu, w_down)


def _experts_kernel(te_ref, na_ref, xs_ref, wg_ref, wu_ref, wd_ref, ys_ref, h_scr, acc_scr):
    i = pl.program_id(0)
    f = pl.program_id(1)
    last = pl.num_programs(1) - 1
    active = i < na_ref[0]

    @pl.when(active & (f == 0))
    def _():
        h_scr[...] = xs_ref[...].astype(BF16)
        acc_scr[...] = jnp.zeros_like(acc_scr)

    @pl.when(active)
    def _():
        _swiglu_step(h_scr[...], wg_ref.at[0], wu_ref.at[0], wd_ref.at[0], acc_scr)

    @pl.when(active & (f == last))
    def _():
        ys_ref[...] = acc_scr[...]

    @pl.when(jnp.logical_not(active) & (f == last))
    def _():
        ys_ref[...] = jnp.zeros_like(ys_ref)


def _experts(xs, tile_expert, n_active, w_gu, w_down, tm, tf_pref=512):
    rows, d = xs.shape
    d_ff = w_down.shape[1]
    tf = _tile(d_ff, tf_pref)
    nf = d_ff // tf

    def fidx(i, f, na):
        return jnp.where(i < na[0], f, nf - 1)

    grid_spec = pltpu.PrefetchScalarGridSpec(
        num_scalar_prefetch=2,
        grid=(rows // tm, nf),
        in_specs=[
            pl.BlockSpec((tm, d), lambda i, f, te, na: (i, 0)),
            pl.BlockSpec((1, d, tf), lambda i, f, te, na: (te[i], 0, fidx(i, f, na))),
            pl.BlockSpec((1, d, tf), lambda i, f, te, na: (te[i], 0, fidx(i, f, na) + nf)),
            pl.BlockSpec((1, tf, d), lambda i, f, te, na: (te[i], fidx(i, f, na), 0)),
        ],
        out_specs=pl.BlockSpec((tm, d), lambda i, f, te, na: (i, 0)),
        scratch_shapes=[pltpu.VMEM((tm, d), BF16), pltpu.VMEM((tm, d), F32)],
    )
    return pl.pallas_call(
        _experts_kernel,
        grid_spec=grid_spec,
        out_shape=jax.ShapeDtypeStruct((rows, d), F32),
        compiler_params=_params(("arbitrary", "arbitrary")),
        name="swiglu_experts",
    )(tile_expert, n_active, xs, w_gu, w_gu, w_down)


def _rope(x, cos, sin_signed):
    parts = []
    for s in range(x.shape[1] // HEAD_DIM):
        xs = x[:, s * HEAD_DIM:(s + 1) * HEAD_DIM]
        parts.append(xs * cos + pltpu.roll(xs, HEAD_DIM // 2, axis=1) * sin_signed)
    return jnp.concatenate(parts, axis=1)


def _qkv_kernel(x_ref, g_ref, cos_ref, sin_ref, wq_ref, wk_ref, wv_ref,
                q_ref, kb_ref, vb_ref, k_ref, v_ref, h_scr):
    j = pl.program_id(1)

    @pl.when(j == 0)
    def _():
        h_scr[...] = _rms(x_ref[...], g_ref[0:1, :]).astype(BF16)

    h = h_scr[...]
    cos = cos_ref[...]
    sin = sin_ref[...]
    tm, tn = q_ref.shape
    n_heads, hw = v_ref.shape[1], v_ref.shape[2]
    q_ref[...] = _rope(_dot(h, wq_ref[...]), cos, sin).astype(BF16)
    k = _rope(_dot(h, wk_ref[...]), cos, sin)
    v = _dot(h, wv_ref[...])
    kb_ref[...] = k.astype(BF16)
    vb_ref[...] = v.astype(BF16)
    slabs = 2 * n_heads
    for jj in range(slabs * HEAD_DIM // tn):
        @pl.when(j == jj)
        def _():
            for s in range(tn // HEAD_DIM):
                k_ref[pl.ds(jj * (tn // HEAD_DIM) + s, tm, stride=slabs), :] = (
                    k[:, s * HEAD_DIM:(s + 1) * HEAD_DIM])
            for s in range(tn // hw):
                v_ref[:, jj * (tn // hw) + s, :] = v[:, s * hw:(s + 1) * hw]


def _qkv_layer(x, g4, cos_tab, sin_tab, w_qkv, tm_pref=512, tn_pref=512):
    n, d = x.shape
    width = w_qkv.shape[1] // 3
    tm = _tile(math.gcd(n, cos_tab.shape[0]), tm_pref)
    tn = _tile(width, tn_pref)
    nj = width // tn
    period = cos_tab.shape[0] // tm
    slabs = width // HEAD_DIM
    assert tn % (2 * HEAD_DIM) == 0
    tab = lambda i, j: (i % period, 0)
    blk = lambda i, j: (i, j)
    return pl.pallas_call(
        _qkv_kernel,
        grid=(n // tm, nj),
        in_specs=[
            pl.BlockSpec((tm, d), lambda i, j: (i, 0)),
            pl.BlockSpec((4, d), lambda i, j: (0, 0)),
            pl.BlockSpec((tm, HEAD_DIM), tab),
            pl.BlockSpec((tm, HEAD_DIM), tab),
            pl.BlockSpec((d, tn), lambda i, j: (0, j)),
            pl.BlockSpec((d, tn), lambda i, j: (0, j + nj)),
            pl.BlockSpec((d, tn), lambda i, j: (0, j + 2 * nj)),
        ],
        out_specs=[pl.BlockSpec((tm, tn), blk)] * 3 + [
            pl.BlockSpec((tm * slabs, HEAD_DIM), lambda i, j: (i, 0)),
            pl.BlockSpec((tm, slabs // 2, 2 * HEAD_DIM), lambda i, j: (i, 0, 0)),
        ],
        out_shape=[jax.ShapeDtypeStruct((n, width), BF16)] * 3 + [
            jax.ShapeDtypeStruct((n * slabs, HEAD_DIM), F32),
            jax.ShapeDtypeStruct((n, slabs // 2, 2 * HEAD_DIM), F32),
        ],
        scratch_shapes=[pltpu.VMEM((tm, d), BF16)],
        compiler_params=_params(("parallel", "arbitrary")),
        name="qkv_rope",
    )(x, g4, cos_tab, sin_tab, w_qkv, w_qkv, w_qkv)


def _lambda(lam_ref, lam_init):
    lp = lam_ref[...]
    a = jnp.sum(lp[0:1, :] * lp[1:2, :], axis=-1, keepdims=True)
    b = jnp.sum(lp[2:3, :] * lp[3:4, :], axis=-1, keepdims=True)
    return jnp.exp(a) - jnp.exp(b) + lam_init


def _softmax_update(s, v_bf16, m_ref, l_ref, acc_ref):
    m_prev = m_ref[...]
    m_new = jnp.maximum(m_prev, jnp.max(s, axis=-1, keepdims=True))
    alpha = jnp.exp(m_prev - m_new)
    p = jnp.exp(s - m_new)
    l_ref[...] = alpha * l_ref[...] + jnp.sum(p, axis=-1, keepdims=True)
    acc_ref[...] = alpha * acc_ref[...] + _dot(p.astype(BF16), v_bf16)
    m_ref[...] = m_new


def _prompt_attn_kernel(lam_ref, sub_ref, q_ref, k_ref, v_ref, o_ref,
                        m_scr, l_scr, acc_scr, *, lam_init, scale):
    qi = pl.program_id(2)
    tq = q_ref.shape[0]

    m_scr[...] = jnp.full_like(m_scr, NEG)
    l_scr[...] = jnp.zeros_like(l_scr)
    acc_scr[...] = jnp.zeros_like(acc_scr)

    def block(j, masked):
        rows = pl.ds(pl.multiple_of(j * tq, tq), tq)
        parts = []
        for c in range(2):
            cs = slice(c * HEAD_DIM, (c + 1) * HEAD_DIM)
            s = _dot_nt(q_ref[:, cs], k_ref[rows, cs]) * scale
            if masked:
                r_id = lax.broadcasted_iota(I32, s.shape, 0)
                c_id = lax.broadcasted_iota(I32, s.shape, 1)
                s = jnp.where(c_id <= r_id, s, NEG)
            parts.append(s)
        _softmax_update(jnp.concatenate(parts, axis=0), v_ref[rows, :], m_scr, l_scr, acc_scr)

    def body(j, carry):
        block(j, False)
        return carry

    lax.fori_loop(0, qi, body, 0)
    block(qi, True)

    lam = _lambda(lam_ref, lam_init)
    o = acc_scr[...] / l_scr[...]
    o = o[:tq] - lam * o[tq:]
    o_ref[...] = (_rms(o, sub_ref[...]) * (1.0 - lam_init)).astype(BF16)


def _prompt_attn(q, k, v, b_lam, subln, batch, seq, lam_init, tq_pref=512):
    n_heads = q.shape[1] // (2 * HEAD_DIM)
    hw = 2 * HEAD_DIM
    tq = _tile(seq, tq_pref)
    nq = seq // tq
    kv_spec = pl.BlockSpec((seq, hw), lambda b, h, i: (b, h))
    return pl.pallas_call(
        functools.partial(_prompt_attn_kernel, lam_init=lam_init, scale=HEAD_DIM ** -0.5),
        grid=(batch, n_heads, nq),
        in_specs=[
            pl.BlockSpec((4, HEAD_DIM), lambda b, h, i: (0, 0)),
            pl.BlockSpec((1, hw), lambda b, h, i: (0, 0)),
            pl.BlockSpec((tq, hw), lambda b, h, i: (b * nq + i, h)),
            kv_spec,
            kv_spec,
        ],
        out_specs=pl.BlockSpec((tq, hw), lambda b, h, i: (b * nq + i, h)),
        out_shape=jax.ShapeDtypeStruct((batch * seq, n_heads * hw), BF16),
        scratch_shapes=[
            pltpu.VMEM((2 * tq, 1), F32),
            pltpu.VMEM((2 * tq, 1), F32),
            pltpu.VMEM((2 * tq, hw), F32),
        ],
        compiler_params=_params(("parallel", "parallel", "arbitrary")),
        name="diff_attn_prompt",
    )(b_lam, subln, q, k, v)


def _decode_attn_kernel(pt_ref, lam_ref, sub_ref, q_ref, kn_ref, vn_ref, *rest,
                        pages_per_step, dec_seq, lam_init, scale):
    k_refs = rest[:pages_per_step]
    v_refs = rest[pages_per_step:2 * pages_per_step]
    o_ref, m_scr, l_scr, acc_scr = rest[2 * pages_per_step:]
    del pt_ref
    step = pl.program_id(1)
    n_heads = v_refs[0].shape[1]
    hw = 2 * HEAD_DIM
    grp = 2 * SUBLANES
    slabs = 2 * n_heads
    page = k_refs[0].shape[0] // slabs
    rows = n_heads * grp

    @pl.when(step == 0)
    def _():
        m_scr[...] = jnp.full_like(m_scr, NEG)
        l_scr[...] = jnp.zeros_like(l_scr)
        acc_scr[...] = jnp.zeros_like(acc_scr)

    def head_keys(k_ref, h):
        return jnp.concatenate([k_ref[pl.ds(2 * h + c, page, stride=slabs), :] for c in range(2)],
                               axis=1).astype(BF16)

    s = jnp.concatenate(
        [jnp.concatenate([_dot_nt(q_ref[0, :, h * hw:(h + 1) * hw], head_keys(k_ref, h))
                          for k_ref in k_refs], axis=1)
         for h in range(n_heads)], axis=0) * scale
    m_prev = m_scr[...]
    m_new = jnp.maximum(m_prev, jnp.max(s, axis=-1, keepdims=True))
    alpha = jnp.exp(m_prev - m_new)
    p = jnp.exp(s - m_new)
    l_scr[...] = alpha * l_scr[...] + jnp.sum(p, axis=-1, keepdims=True)
    pb = p.astype(BF16)
    pv = jnp.concatenate(
        [_dot(pb[h * grp:(h + 1) * grp, :],
              jnp.concatenate([v_ref[:, h, :] for v_ref in v_refs], axis=0).astype(BF16))
         for h in range(n_heads)], axis=0)
    acc_scr[...] = alpha * acc_scr[...] + pv
    m_scr[...] = m_new

    @pl.when(step == pl.num_programs(1) - 1)
    def _():
        lam = _lambda(lam_ref, lam_init)
        lane = lax.broadcasted_iota(I32, (rows, LANES), 1)
        q_id = lax.broadcasted_iota(I32, (rows, LANES), 0) % SUBLANES
        qf = jnp.concatenate([q_ref[0, :, h * hw:(h + 1) * hw] for h in range(n_heads)],
                             axis=0).astype(F32)

        def per_head(ref, t):
            return jnp.concatenate(
                [jnp.broadcast_to(ref[0, t:t + 1, h * hw:(h + 1) * hw], (grp, hw))
                 for h in range(n_heads)], axis=0)

        sn = jnp.full((rows, LANES), NEG, F32)
        for t in range(dec_seq):
            st = jnp.sum(qf * per_head(kn_ref, t), axis=-1, keepdims=True) * scale
            sn = jnp.where((lane == t) & (t <= q_id), st, sn)
        m_old = m_scr[...]
        m_fin = jnp.maximum(m_old, jnp.max(sn, axis=-1, keepdims=True))
        a_fin = jnp.exp(m_old - m_fin)
        pn = jnp.exp(sn - m_fin)
        l_fin = a_fin * l_scr[...] + jnp.sum(pn, axis=-1, keepdims=True)
        acc = a_fin * acc_scr[...]
        for t in range(dec_seq):
            pt = jnp.sum(jnp.where(lane == t, pn, 0.0), axis=-1, keepdims=True)
            acc = acc + pt * per_head(vn_ref, t)
        o = acc / l_fin
        for h in range(n_heads):
            oh = o[h * grp:h * grp + SUBLANES] - lam * o[h * grp + SUBLANES:(h + 1) * grp]
            o_ref[0, :, h * hw:(h + 1) * hw] = (_rms(oh, sub_ref[...]) * (1.0 - lam_init)).astype(BF16)


def _decode_queries(q_rows, dec_batch, dec_seq):
    d = q_rows.shape[1]
    qs = q_rows.reshape(dec_batch, dec_seq, d // (2 * HEAD_DIM), 2, HEAD_DIM)
    qs = jnp.pad(qs, ((0, 0), (0, SUBLANES - dec_seq), (0, 0), (0, 0), (0, 0)))
    q16 = jnp.einsum("bqhcd,ce->bcqhed", qs, jnp.eye(2, dtype=qs.dtype))
    return q16.reshape(dec_batch, 2 * SUBLANES, d)


def _decode_attn(q16, kn8, vn8, cache_k, cache_v, page_table, page, b_lam, subln, dec_seq, lam_init,
                 pages_per_step=4):
    dec_batch, _, width = q16.shape
    n_pages = page_table.shape[1]
    n_heads = cache_v.shape[1]
    hw = 2 * HEAD_DIM
    pps = pages_per_step if n_pages % pages_per_step == 0 else 1
    rows = n_heads * 2 * SUBLANES

    def k_spec(u):
        return pl.BlockSpec((page * 2 * n_heads, HEAD_DIM),
                            lambda s, p, pt: (pt[s * n_pages + p * pps + u], 0))

    def v_spec(u):
        return pl.BlockSpec((page, n_heads, hw),
                            lambda s, p, pt: (pt[s * n_pages + p * pps + u], 0, 0))

    def seq_spec(r):
        return pl.BlockSpec((1, r, width), lambda s, p, pt: (s, 0, 0))

    grid_spec = pltpu.PrefetchScalarGridSpec(
        num_scalar_prefetch=1,
        grid=(dec_batch, n_pages // pps),
        in_specs=[
            pl.BlockSpec((4, HEAD_DIM), lambda s, p, pt: (0, 0)),
            pl.BlockSpec((1, hw), lambda s, p, pt: (0, 0)),
            seq_spec(2 * SUBLANES), seq_spec(SUBLANES), seq_spec(SUBLANES),
        ] + [k_spec(u) for u in range(pps)] + [v_spec(u) for u in range(pps)],
        out_specs=seq_spec(SUBLANES),
        scratch_shapes=[
            pltpu.VMEM((rows, 1), F32),
            pltpu.VMEM((rows, 1), F32),
            pltpu.VMEM((rows, hw), F32),
        ],
    )
    return pl.pallas_call(
        functools.partial(_decode_attn_kernel, pages_per_step=pps, dec_seq=dec_seq,
                          lam_init=lam_init, scale=HEAD_DIM ** -0.5),
        grid_spec=grid_spec,
        out_shape=jax.ShapeDtypeStruct((dec_batch, SUBLANES, width), BF16),
        compiler_params=_params(("parallel", "arbitrary")),
        name="diff_attn_decode",
    )(page_table.reshape(-1), b_lam, subln, q16, kn8, vn8,
      *([cache_k] * pps), *([cache_v] * pps))


def _wo_router_kernel(o_ref, x_ref, g_ref, wo_ref, wr_ref, br_ref, tri_ref, cnt0_ref,
                      x3_ref, hm_ref, meta_ref, cnt_ref, carry_scr, *, n_experts):
    i = pl.program_id(0)

    @pl.when(i == 0)
    def _():
        carry_scr[...] = cnt0_ref[...]

    x3 = x_ref[...] + _rms(_dot(o_ref[...], wo_ref[...]), g_ref[1:2, :])
    x3_ref[...] = x3
    hm = _rms(x3, g_ref[2:3, :])
    hm_ref[...] = hm
    logits = jnp.dot(hm, wr_ref[...], preferred_element_type=F32,
                     precision=lax.Precision.HIGHEST) + br_ref[...]
    lane = lax.broadcasted_iota(I32, logits.shape, 1)
    logits = jnp.where(lane < n_experts, logits, -jnp.inf)
    m1 = jnp.max(logits, axis=-1, keepdims=True)
    i1 = jnp.min(jnp.where(logits == m1, lane, LANES), axis=-1, keepdims=True)
    rest = jnp.where(lane == i1, -jnp.inf, logits)
    m2 = jnp.max(rest, axis=-1, keepdims=True)
    i2 = jnp.min(jnp.where(rest == m2, lane, LANES), axis=-1, keepdims=True)
    e2 = jnp.exp(m2 - m1)
    g1 = 1.0 / (1.0 + e2)
    g2 = e2 / (1.0 + e2)
    oh1 = (lane == i1).astype(F32)
    oh2 = (lane == i2).astype(F32)
    tri = tri_ref[...]
    before1 = _dot(tri, oh1.astype(BF16))
    before2 = _dot(tri, oh2.astype(BF16))
    tot1 = jnp.sum(oh1, axis=0, keepdims=True)
    tot2 = jnp.sum(oh2, axis=0, keepdims=True)
    carry = carry_scr[...]
    r1 = jnp.sum(oh1 * (carry + before1), axis=-1, keepdims=True)
    r2 = jnp.sum(oh2 * (carry + tot1 + before2), axis=-1, keepdims=True)
    carry = carry + tot1 + tot2
    carry_scr[...] = carry
    cnt_ref[...] = carry
    meta = jnp.zeros(logits.shape, F32)
    for col, val in enumerate((i1.astype(F32), i2.astype(F32), g1, g2, r1, r2)):
        meta = jnp.where(lane == col, val, meta)
    meta_ref[...] = meta


def _wo_router(o, x, g4, w_o, w_r, b_r, cnt0, n_experts, tm_pref=512):
    n, d = x.shape
    tm = _tile(n, tm_pref)
    row = lambda i: (i, 0)
    tri = jnp.tril(jnp.ones((tm, tm), BF16), k=-1)
    return pl.pallas_call(
        functools.partial(_wo_router_kernel, n_experts=n_experts),
        grid=(n // tm,),
        in_specs=[
            pl.BlockSpec((tm, d), row),
            pl.BlockSpec((tm, d), row),
            _resident((4, d)),
            _resident(w_o.shape),
            _resident(w_r.shape),
            _resident(b_r.shape),
            _resident((tm, tm)),
            _resident((1, LANES)),
        ],
        out_specs=[
            pl.BlockSpec((tm, d), row),
            pl.BlockSpec((tm, d), row),
            pl.BlockSpec((tm, LANES), row),
            pl.BlockSpec((1, LANES), lambda i: (0, 0)),
        ],
        out_shape=[
            jax.ShapeDtypeStruct((n, d), F32),
            jax.ShapeDtypeStruct((n, d), F32),
            jax.ShapeDtypeStruct((n, LANES), F32),
            jax.ShapeDtypeStruct((1, LANES), F32),
        ],
        scratch_shapes=[pltpu.VMEM((1, LANES), F32)],
        compiler_params=_params(("arbitrary",)),
        name="wo_router",
    )(o, x, g4, w_o, w_r, b_r, tri, cnt0)


def _row_copy(src_ref, src_row, dst_ref, dst_row, sem):
    return pltpu.make_async_copy(src_ref.at[pl.ds(src_row, 1), :], dst_ref.at[pl.ds(dst_row, 1), :], sem)


def _dispatch_kernel(pos_ref, hm_ref, init_ref, xs_ref, sem):
    del init_ref
    tm = hm_ref.shape[0]
    base = pl.program_id(0) * tm * TOP_K

    def start(r, carry):
        for k in range(TOP_K):
            _row_copy(hm_ref, r, xs_ref, pos_ref[base + r * TOP_K + k], sem).start()
        return carry

    def wait(r, carry):
        for k in range(TOP_K):
            _row_copy(hm_ref, r, xs_ref, pos_ref[base + r * TOP_K + k], sem).wait()
        return carry

    lax.fori_loop(0, tm, start, 0)
    lax.fori_loop(0, tm, wait, 0)


def _dispatch(hm, pos_flat, xs, tm_pref=512):
    n, d = hm.shape
    tm = _tile(n, tm_pref)
    grid_spec = pltpu.PrefetchScalarGridSpec(
        num_scalar_prefetch=1,
        grid=(n // tm,),
        in_specs=[
            pl.BlockSpec((tm, d), lambda i, pos: (i, 0)),
            pl.BlockSpec(memory_space=pl.ANY),
        ],
        out_specs=pl.BlockSpec(memory_space=pl.ANY),
        scratch_shapes=[pltpu.SemaphoreType.DMA(())],
    )
    return pl.pallas_call(
        _dispatch_kernel,
        grid_spec=grid_spec,
        out_shape=jax.ShapeDtypeStruct(xs.shape, F32),
        input_output_aliases={2: 0},
        compiler_params=_params(("arbitrary",)),
        name="moe_dispatch",
    )(pos_flat, hm, xs)


def _combine_kernel(pos_ref, x_ref, g_ref, meta_ref, ys_ref, out_ref, a_scr, b_scr, sem):
    tm = x_ref.shape[0]
    base = pl.program_id(0) * tm * TOP_K
    bufs = (a_scr, b_scr)

    def start(r, carry):
        for k in range(TOP_K):
            _row_copy(ys_ref, pos_ref[base + r * TOP_K + k], bufs[k], r, sem).start()
        return carry

    def wait(r, carry):
        for k in range(TOP_K):
            _row_copy(ys_ref, pos_ref[base + r * TOP_K + k], bufs[k], r, sem).wait()
        return carry

    lax.fori_loop(0, tm, start, 0)
    lax.fori_loop(0, tm, wait, 0)
    meta = meta_ref[...]
    f = meta[:, 2:3] * a_scr[...] + meta[:, 3:4] * b_scr[...]
    out_ref[...] = x_ref[...] + _rms(f, g_ref[3:4, :])


def _combine(x, g4, meta, ys, pos_flat, tm_pref=512):
    n, d = x.shape
    tm = _tile(n, tm_pref)
    grid_spec = pltpu.PrefetchScalarGridSpec(
        num_scalar_prefetch=1,
        grid=(n // tm,),
        in_specs=[
            pl.BlockSpec((tm, d), lambda i, pos: (i, 0)),
            pl.BlockSpec((4, d), lambda i, pos: (0, 0)),
            pl.BlockSpec((tm, LANES), lambda i, pos: (i, 0)),
            pl.BlockSpec(memory_space=pl.ANY),
        ],
        out_specs=pl.BlockSpec((tm, d), lambda i, pos: (i, 0)),
        scratch_shapes=[pltpu.VMEM((tm, d), F32), pltpu.VMEM((tm, d), F32),
                        pltpu.SemaphoreType.DMA(())],
    )
    return pl.pallas_call(
        _combine_kernel,
        grid_spec=grid_spec,
        out_shape=jax.ShapeDtypeStruct((n, d), F32),
        compiler_params=_params(("arbitrary",)),
        name="moe_combine",
    )(pos_flat, x, g4, meta, ys)


def _rope_tables(pos):
    inv = jnp.exp(-math.log(ROPE_THETA) * jnp.arange(0, HEAD_DIM, 2, dtype=F32) / HEAD_DIM)
    ang = pos.astype(F32)[:, None] * inv[None, :]
    ang = jnp.concatenate([ang, ang], axis=-1)
    sign = jnp.where(jnp.arange(HEAD_DIM) < HEAD_DIM // 2, -1.0, 1.0).astype(F32)
    return jnp.cos(ang), jnp.sin(ang) * sign


def _mix_tables(w_s, b_s, dec_seq):
    causal = jnp.tril(jnp.ones((CHUNK, CHUNK), dtype=bool))
    w_prompt = jnp.where(causal[None], w_s, 0)
    per = CHUNK // dec_seq
    w_small = w_prompt[:, :dec_seq, :dec_seq]
    eye = jnp.eye(per, dtype=w_s.dtype)
    w_sample = jnp.einsum("ab,gts->gatbs", eye, w_small).reshape(N_GROUPS, CHUNK, CHUNK)
    gd = LANES
    b_prompt = jnp.repeat(jnp.transpose(b_s), gd, axis=1)
    b_sample = jnp.repeat(jnp.tile(jnp.transpose(b_s)[:dec_seq], (per, 1)), gd, axis=1)
    return (w_prompt.astype(BF16), b_prompt), (w_sample.astype(BF16), b_sample)


def kernel(x_prompt, x_sample, cache_k, cache_v, page_table, norm_g, a_w_in, a_ln_g, a_ln_b, a_w_s,
           a_b_s, a_w_out, f_w_gu, f_w_down, b_w_qkv, b_lam, b_subln, b_w_o, e_w_r, e_b_r, e_w_gu,
           e_w_down):
    batch, seq, d = x_prompt.shape
    dec_batch, dec_seq, _ = x_sample.shape
    assert norm_g.shape[0] == 2 and a_w_in.shape[0] == 1 and b_w_qkv.shape[0] == 1
    n_p, n_s = batch * seq, dec_batch * dec_seq
    n = n_p + n_s
    n_experts = e_w_r.shape[-1]
    n_heads = d // (2 * HEAD_DIM)
    page = cache_k.shape[2]
    past = page_table.shape[1] * page
    assert CHUNK % dec_seq == 0 and dec_seq <= SUBLANES
    tm_e = 512
    xp, xs_ = x_prompt.reshape(n_p, d), x_sample.reshape(n_s, d)

    mix_p, mix_s = _mix_tables(a_w_s[0], a_b_s[0], dec_seq)
    w_in, w_out = a_w_in[0].astype(BF16), a_w_out[0].astype(BF16)
    w_gu, w_down = f_w_gu[0].astype(BF16), f_w_down[0].astype(BF16)

    def layer0(x, mix):
        x, v_rows = _gmlp_layer(x, norm_g[0], w_in, a_ln_g, a_ln_b, mix[0], mix[1], w_out)
        return _swiglu_layer(x, norm_g[0], w_gu, w_down), v_rows

    xp, _ = layer0(xp, mix_p)
    xs_, v_rows = layer0(xs_, mix_s)

    lam_init = 0.8 - 0.6 * math.exp(-0.3 * 1)
    w_qkv = b_w_qkv[0].astype(BF16)
    tab_rows = _tile(n_s, 512)
    assert tab_rows % dec_seq == 0
    q_p, kb_p, vb_p, k_p, v_p = _qkv_layer(xp, norm_g[1], *_rope_tables(jnp.arange(seq)), w_qkv)
    q_s, kb_s, vb_s, k_s, v_s = _qkv_layer(
        xs_, norm_g[1], *_rope_tables(past + jnp.arange(tab_rows) % dec_seq), w_qkv)
    o_p = _prompt_attn(q_p, kb_p, vb_p, b_lam[0], b_subln, batch, seq, lam_init)
    pad8 = lambda a: jnp.pad(a.reshape(dec_batch, dec_seq, d).astype(F32),
                             ((0, 0), (0, SUBLANES - dec_seq), (0, 0)))
    o_s = _decode_attn(_decode_queries(q_s, dec_batch, dec_seq), pad8(kb_s), pad8(vb_s),
                       cache_k.reshape(-1, HEAD_DIM), cache_v.reshape(-1, n_heads, 2 * HEAD_DIM),
                       page_table, page, b_lam[0], b_subln, dec_seq, lam_init)
    o_s = o_s[:, :dec_seq].reshape(n_s, d)

    w_o = b_w_o[0].astype(BF16)
    w_r = jnp.pad(e_w_r[0], ((0, 0), (0, LANES - n_experts)))
    b_r = jnp.pad(e_b_r, ((0, 0), (0, LANES - n_experts)))
    cnt = jnp.zeros((1, LANES), F32)
    xp, hm_p, meta_p, cnt = _wo_router(o_p, xp, norm_g[1], w_o, w_r, b_r, cnt, n_experts)
    xs_, hm_s, meta_s, cnt = _wo_router(o_s, xs_, norm_g[1], w_o, w_r, b_r, cnt, n_experts)
    counts = cnt[0, :n_experts].astype(I32)
    tiles_per = (counts + tm_e - 1) // tm_e
    tile_end = jnp.cumsum(tiles_per)
    starts = (tile_end - tiles_per) * tm_e

    def slots(meta):
        expert = meta[:, 0:TOP_K].astype(I32)
        rank = meta[:, 4:4 + TOP_K].astype(I32)
        return (starts[expert] + rank).reshape(-1)

    pos_p, pos_s = slots(meta_p), slots(meta_s)
    n_tiles = (TOP_K * (n_p + n_s) + n_experts * (tm_e - 1)) // tm_e
    n_active = tile_end[-1:]
    tile_id = jnp.minimum(jnp.arange(n_tiles), n_active[0] - 1)
    tile_expert = jnp.sum(tile_id[:, None] >= tile_end[None, :], axis=1).astype(I32)
    xs = jnp.zeros((n_tiles * tm_e, d), F32)
    xs = _dispatch(hm_p, pos_p, xs)
    xs = _dispatch(hm_s, pos_s, xs)
    ys = _experts(xs, tile_expert, n_active.astype(I32), e_w_gu[0].astype(BF16),
                  e_w_down[0].astype(BF16), tm_e)
    yp = _combine(xp, norm_g[1], meta_p, ys, pos_p)
    ysm = _combine(xs_, norm_g[1], meta_s, ys, pos_s)

    return (
        yp.reshape(batch, seq, d),
        ysm.reshape(dec_batch, dec_seq, d),
        k_p.reshape(1, batch, seq, n_heads, 2, HEAD_DIM),
        v_p.reshape(1, batch, seq, n_heads, 2 * HEAD_DIM),
        k_s.reshape(1, dec_batch, dec_seq, n_heads, 2, HEAD_DIM),
        v_s.reshape(1, dec_batch, dec_seq, n_heads, 2 * HEAD_DIM),
        v_rows.reshape(1, dec_batch, dec_seq, -1),
    )
```

```python
import functools
import math

import jax
import jax.numpy as jnp
from jax import lax
from jax.experimental import pallas as pl
from jax.experimental.pallas import tpu as pltpu

F32 = jnp.float32
BF16 = jnp.bfloat16
I32 = jnp.int32

EPS = 1e-6
CHUNK = 128
N_GROUPS = 16
HEAD_DIM = 128
ROPE_THETA = 10000.0
TOP_K = 2
NEG = -1e30
LANES = 128
SUBLANES = 8
VMEM_LIMIT = 56 * 1024 * 1024
TILE = 512


def _dot(a, b):
    return jnp.dot(a, b, preferred_element_type=F32)


def _dot_nt(a, b):
    return lax.dot_general(a, b, (((1,), (1,)), ((), ())), preferred_element_type=F32)


def _rms(x, g):
    return x * lax.rsqrt(jnp.mean(x * x, axis=-1, keepdims=True) + EPS) * g


def _tile(n, pref):
    t = min(n, pref)
    while n % t:
        t -= SUBLANES
    assert t > 0 and t % SUBLANES == 0, (n, pref)
    return t


def _params(sem, vmem=VMEM_LIMIT):
    return pltpu.CompilerParams(dimension_semantics=sem, vmem_limit_bytes=vmem)


def _resident(shape):
    return pl.BlockSpec(shape, lambda *_: (0,) * len(shape), pipeline_mode=pl.Buffered(1))


def _gmlp_kernel(x_ref, g_ref, win_ref, lng_ref, lnb_ref, wmix_ref, bias_ref, wout_ref,
                 out_ref, v_ref, h_scr, u_scr, vb_scr, y_scr, *, col_w):
    tm, d = x_ref.shape
    d_inner = u_scr.shape[1]
    x = x_ref[...]
    h_scr[...] = _rms(x, g_ref[0:1, :]).astype(BF16)
    n_col = d_inner // col_w
    s1 = jnp.zeros((tm, 1), F32)
    for c in range(n_col):
        cs = slice(c * col_w, (c + 1) * col_w)
        u_scr[:, cs] = jax.nn.gelu(_dot(h_scr[...], win_ref[:, cs]))
        v = jax.nn.gelu(_dot(h_scr[...], win_ref[:, d_inner + c * col_w:d_inner + (c + 1) * col_w]))
        v_ref[:, cs] = v
        s1 = s1 + jnp.sum(v, axis=-1, keepdims=True)
    mu = s1 / d_inner
    s2 = jnp.zeros((tm, 1), F32)
    for c in range(n_col):
        cs = slice(c * col_w, (c + 1) * col_w)
        dv = v_ref[:, cs] - mu
        s2 = s2 + jnp.sum(dv * dv, axis=-1, keepdims=True)
    rstd = lax.rsqrt(s2 / d_inner + EPS)
    for c in range(n_col):
        cs = slice(c * col_w, (c + 1) * col_w)
        vn = (v_ref[:, cs] - mu) * rstd * lng_ref[:, cs] + lnb_ref[:, cs]
        v_ref[:, cs] = vn
        vb_scr[:, cs] = vn.astype(BF16)
    gd = d_inner // N_GROUPS
    for r in range(tm // CHUNK):
        rs = slice(r * CHUNK, (r + 1) * CHUNK)
        for g in range(N_GROUPS):
            gs = slice(g * gd, (g + 1) * gd)
            z = _dot(wmix_ref[g], vb_scr[rs, gs]) + bias_ref[:, gs]
            y_scr[rs, gs] = (u_scr[rs, gs] * z).astype(BF16)
    m = _dot(y_scr[...], wout_ref[...])
    out_ref[...] = x + _rms(m, g_ref[1:2, :])


def _gmlp_layer(x, g4, w_in, ln_g, ln_b, wmix, bias, w_out):
    n, d = x.shape
    d_inner = w_out.shape[0]
    tm = _tile(n, 2 * CHUNK)
    assert tm % CHUNK == 0 and d_inner // N_GROUPS == LANES
    row = lambda i: (i, 0)
    return pl.pallas_call(
        functools.partial(_gmlp_kernel, col_w=_tile(d_inner, TILE)),
        grid=(n // tm,),
        in_specs=[
            pl.BlockSpec((tm, d), row),
            _resident((4, d)),
            _resident(w_in.shape),
            _resident((1, d_inner)),
            _resident((1, d_inner)),
            _resident(wmix.shape),
            _resident(bias.shape),
            _resident(w_out.shape),
        ],
        out_specs=[pl.BlockSpec((tm, d), row), pl.BlockSpec((tm, d_inner), row)],
        out_shape=[jax.ShapeDtypeStruct((n, d), F32), jax.ShapeDtypeStruct((n, d_inner), F32)],
        scratch_shapes=[
            pltpu.VMEM((tm, d), BF16),
            pltpu.VMEM((tm, d_inner), F32),
            pltpu.VMEM((tm, d_inner), BF16),
            pltpu.VMEM((tm, d_inner), BF16),
        ],
        compiler_params=_params(("parallel",)),
        name="gmlp_layer",
    )(x, g4, w_in, ln_g, ln_b, wmix, bias, w_out)


def _swiglu_step(h, wg_ref, wu_ref, wd_ref, acc_ref):
    g = _dot(h, wg_ref[...])
    u = _dot(h, wu_ref[...])
    a = (jax.nn.silu(g) * u).astype(BF16)
    acc_ref[...] += _dot(a, wd_ref[...])


def _swiglu_kernel(x_ref, g_ref, wg_ref, wu_ref, wd_ref, out_ref, h_scr, acc_scr):
    f = pl.program_id(1)

    @pl.when(f == 0)
    def _():
        h_scr[...] = _rms(x_ref[...], g_ref[2:3, :]).astype(BF16)
        acc_scr[...] = jnp.zeros_like(acc_scr)

    _swiglu_step(h_scr[...], wg_ref, wu_ref, wd_ref, acc_scr)

    @pl.when(f == pl.num_programs(1) - 1)
    def _():
        out_ref[...] = x_ref[...] + _rms(acc_scr[...], g_ref[3:4, :])


def _swiglu_layer(x, g4, w_gu, w_down, tm_pref=TILE, tf_pref=TILE):
    n, d = x.shape
    d_ff = w_down.shape[0]
    tm = _tile(n, tm_pref)
    tf = _tile(d_ff, tf_pref)
    nf = d_ff // tf
    return pl.pallas_call(
        _swiglu_kernel,
        grid=(n // tm, nf),
        in_specs=[
            pl.BlockSpec((tm, d), lambda i, f: (i, 0)),
            pl.BlockSpec((4, d), lambda i, f: (0, 0)),
            pl.BlockSpec((d, tf), lambda i, f: (0, f)),
            pl.BlockSpec((d, tf), lambda i, f: (0, f + nf)),
            pl.BlockSpec((tf, d), lambda i, f: (f, 0)),
        ],
        out_specs=pl.BlockSpec((tm, d), lambda i, f: (i, 0)),
        out_shape=jax.ShapeDtypeStruct((n, d), F32),
        scratch_shapes=[pltpu.VMEM((tm, d), BF16), pltpu.VMEM((tm, d), F32)],
        compiler_params=_params(("parallel", "arbitrary")),
        name="swiglu_dense",
    )(x, g4, w_gu, w_gu, w_down)


def _experts_kernel(te_ref, na_ref, xs_ref, wg_ref, wu_ref, wd_ref, ys_ref, h_scr, acc_scr):
    i = pl.program_id(0)
    f = pl.program_id(1)
    last = pl.num_programs(1) - 1
    active = i < na_ref[0]

    @pl.when(active & (f == 0))
    def _():
        h_scr[...] = xs_ref[...].astype(BF16)
        acc_scr[...] = jnp.zeros_like(acc_scr)

    @pl.when(active)
    def _():
        _swiglu_step(h_scr[...], wg_ref.at[0], wu_ref.at[0], wd_ref.at[0], acc_scr)

    @pl.when(active & (f == last))
    def _():
        ys_ref[...] = acc_scr[...]

    @pl.when(jnp.logical_not(active) & (f == last))
    def _():
        ys_ref[...] = jnp.zeros_like(ys_ref)


def _experts(xs, tile_expert, n_active, w_gu, w_down, tm, tf_pref=TILE):
    rows, d = xs.shape
    d_ff = w_down.shape[1]
    tf = _tile(d_ff, tf_pref)
    nf = d_ff // tf

    def fidx(i, f, na):
        return jnp.where(i < na[0], f, nf - 1)

    grid_spec = pltpu.PrefetchScalarGridSpec(
        num_scalar_prefetch=2,
        grid=(rows // tm, nf),
        in_specs=[
            pl.BlockSpec((tm, d), lambda i, f, te, na: (i, 0)),
            pl.BlockSpec((1, d, tf), lambda i, f, te, na: (te[i], 0, fidx(i, f, na))),
            pl.BlockSpec((1, d, tf), lambda i, f, te, na: (te[i], 0, fidx(i, f, na) + nf)),
            pl.BlockSpec((1, tf, d), lambda i, f, te, na: (te[i], fidx(i, f, na), 0)),
        ],
        out_specs=pl.BlockSpec((tm, d), lambda i, f, te, na: (i, 0)),
        scratch_shapes=[pltpu.VMEM((tm, d), BF16), pltpu.VMEM((tm, d), F32)],
    )
    return pl.pallas_call(
        _experts_kernel,
        grid_spec=grid_spec,
        out_shape=jax.ShapeDtypeStruct((rows, d), F32),
        compiler_params=_params(("arbitrary", "arbitrary")),
        name="swiglu_experts",
    )(tile_expert, n_active, xs, w_gu, w_gu, w_down)


def _rope(x, cos, sin_signed):
    parts = []
    for s in range(x.shape[1] // HEAD_DIM):
        xs = x[:, s * HEAD_DIM:(s + 1) * HEAD_DIM]
        parts.append(xs * cos + pltpu.roll(xs, HEAD_DIM // 2, axis=1) * sin_signed)
    return jnp.concatenate(parts, axis=1)


def _qkv_kernel(x_ref, g_ref, cos_ref, sin_ref, wq_ref, wk_ref, wv_ref,
                q_ref, kb_ref, vb_ref, k_ref, v_ref, h_scr):
    j = pl.program_id(1)

    @pl.when(j == 0)
    def _():
        h_scr[...] = _rms(x_ref[...], g_ref[0:1, :]).astype(BF16)

    h = h_scr[...]
    cos = cos_ref[...]
    sin = sin_ref[...]
    tm, tn = q_ref.shape
    n_heads, hw = v_ref.shape[1], v_ref.shape[2]
    q_ref[...] = _rope(_dot(h, wq_ref[...]), cos, sin).astype(BF16)
    k = _rope(_dot(h, wk_ref[...]), cos, sin)
    v = _dot(h, wv_ref[...])
    kb_ref[...] = k.astype(BF16)
    vb_ref[...] = v.astype(BF16)
    slabs = 2 * n_heads
    for jj in range(slabs * HEAD_DIM // tn):
        @pl.when(j == jj)
        def _():
            for s in range(tn // HEAD_DIM):
                k_ref[pl.ds(jj * (tn // HEAD_DIM) + s, tm, stride=slabs), :] = (
                    k[:, s * HEAD_DIM:(s + 1) * HEAD_DIM])
            for s in range(tn // hw):
                v_ref[:, jj * (tn // hw) + s, :] = v[:, s * hw:(s + 1) * hw]


def _qkv_layer(x, g4, cos_tab, sin_tab, w_qkv, tm_pref=TILE, tn_pref=TILE):
    n, d = x.shape
    width = w_qkv.shape[1] // 3
    tm = _tile(math.gcd(n, cos_tab.shape[0]), tm_pref)
    tn = _tile(width, tn_pref)
    nj = width // tn
    period = cos_tab.shape[0] // tm
    slabs = width // HEAD_DIM
    assert tn % (2 * HEAD_DIM) == 0
    tab = lambda i, j: (i % period, 0)
    blk = lambda i, j: (i, j)
    return pl.pallas_call(
        _qkv_kernel,
        grid=(n // tm, nj),
        in_specs=[
            pl.BlockSpec((tm, d), lambda i, j: (i, 0)),
            pl.BlockSpec((4, d), lambda i, j: (0, 0)),
            pl.BlockSpec((tm, HEAD_DIM), tab),
            pl.BlockSpec((tm, HEAD_DIM), tab),
            pl.BlockSpec((d, tn), lambda i, j: (0, j)),
            pl.BlockSpec((d, tn), lambda i, j: (0, j + nj)),
            pl.BlockSpec((d, tn), lambda i, j: (0, j + 2 * nj)),
        ],
        out_specs=[pl.BlockSpec((tm, tn), blk)] * 3 + [
            pl.BlockSpec((tm * slabs, HEAD_DIM), lambda i, j: (i, 0)),
            pl.BlockSpec((tm, slabs // 2, 2 * HEAD_DIM), lambda i, j: (i, 0, 0)),
        ],
        out_shape=[jax.ShapeDtypeStruct((n, width), BF16)] * 3 + [
            jax.ShapeDtypeStruct((n * slabs, HEAD_DIM), F32),
            jax.ShapeDtypeStruct((n, slabs // 2, 2 * HEAD_DIM), F32),
        ],
        scratch_shapes=[pltpu.VMEM((tm, d), BF16)],
        compiler_params=_params(("parallel", "arbitrary")),
        name="qkv_rope",
    )(x, g4, cos_tab, sin_tab, w_qkv, w_qkv, w_qkv)


def _lambda(lam_ref, lam_init):
    lp = lam_ref[...]
    a = jnp.sum(lp[0:1, :] * lp[1:2, :], axis=-1, keepdims=True)
    b = jnp.sum(lp[2:3, :] * lp[3:4, :], axis=-1, keepdims=True)
    return jnp.exp(a) - jnp.exp(b) + lam_init


def _prompt_attn_kernel(lam_ref, sub_ref, q_ref, k_ref, v_ref, o_ref, *, lam_init, scale, n_q):
    qi = pl.program_id(2)
    tq = q_ref.shape[0]

    def tile_body(t):
        w0 = t * tq
        outs = []
        for c in range(2):
            cs = slice(c * HEAD_DIM, (c + 1) * HEAD_DIM)
            q = q_ref[:, cs]
            sd = _dot_nt(q, k_ref[w0:w0 + tq, cs]) * scale
            r_id = lax.broadcasted_iota(I32, sd.shape, 0)
            c_id = lax.broadcasted_iota(I32, sd.shape, 1)
            sd = jnp.where(c_id <= r_id, sd, NEG)
            m = jnp.max(sd, axis=-1, keepdims=True)
            if t > 0:
                sm = _dot_nt(q, k_ref[0:w0, cs]) * scale
                m = jnp.maximum(m, jnp.max(sm, axis=-1, keepdims=True))
            pd = jnp.exp(sd - m)
            l = jnp.sum(pd, axis=-1, keepdims=True)
            acc = _dot(pd.astype(BF16), v_ref[w0:w0 + tq, :])
            if t > 0:
                pm = jnp.exp(sm - m)
                l = l + jnp.sum(pm, axis=-1, keepdims=True)
                acc = acc + _dot(pm.astype(BF16), v_ref[0:w0, :])
            outs.append(acc / l)
        lam = _lambda(lam_ref, lam_init)
        o = outs[0] - lam * outs[1]
        o_ref[...] = (_rms(o, sub_ref[...]) * (1.0 - lam_init)).astype(BF16)

    for t in range(n_q):
        pl.when(qi == t)(functools.partial(tile_body, t))


def _prompt_attn(q, k, v, b_lam, subln, batch, seq, lam_init, tq_pref=TILE):
    n_heads = q.shape[1] // (2 * HEAD_DIM)
    hw = 2 * HEAD_DIM
    tq = _tile(seq, tq_pref)
    nq = seq // tq
    kv_spec = pl.BlockSpec((seq, hw), lambda b, h, i: (b, h))
    return pl.pallas_call(
        functools.partial(_prompt_attn_kernel, lam_init=lam_init, scale=HEAD_DIM ** -0.5, n_q=nq),
        grid=(batch, n_heads, nq),
        in_specs=[
            pl.BlockSpec((4, HEAD_DIM), lambda b, h, i: (0, 0)),
            pl.BlockSpec((1, hw), lambda b, h, i: (0, 0)),
            pl.BlockSpec((tq, hw), lambda b, h, i: (b * nq + i, h)),
            kv_spec,
            kv_spec,
        ],
        out_specs=pl.BlockSpec((tq, hw), lambda b, h, i: (b * nq + i, h)),
        out_shape=jax.ShapeDtypeStruct((batch * seq, n_heads * hw), BF16),
        compiler_params=_params(("parallel", "parallel", "arbitrary")),
        name="diff_attn_prompt",
    )(b_lam, subln, q, k, v)


def _decode_attn_kernel(pt_ref, lam_ref, sub_ref, spread_ref, headmask_ref, q_ref, kn_ref, vn_ref,
                        *rest, pages_per_step, n_heads, dec_seq, lam_init, scale):
    k_refs = rest[:pages_per_step]
    v_refs = rest[pages_per_step:2 * pages_per_step]
    o_ref, m_scr, l_scr, acc_scr = rest[2 * pages_per_step:]
    del pt_ref
    step = pl.program_id(1)
    hw = 2 * HEAD_DIM
    grp = 2 * SUBLANES
    slabs = 2 * n_heads
    page = k_refs[0].shape[0] // slabs
    rows = n_heads * grp

    @pl.when(step == 0)
    def _():
        m_scr[...] = jnp.full_like(m_scr, NEG)
        l_scr[...] = jnp.zeros_like(l_scr)
        acc_scr[...] = jnp.zeros_like(acc_scr)

    def head_keys(k_ref, h):
        return jnp.concatenate([k_ref[pl.ds(2 * h + c, page, stride=slabs), :] for c in range(2)],
                               axis=1).astype(BF16)

    s = jnp.concatenate(
        [jnp.concatenate([_dot_nt(q_ref[0, :, h * hw:(h + 1) * hw], head_keys(k_ref, h))
                          for k_ref in k_refs], axis=1)
         for h in range(n_heads)], axis=0) * scale
    m_prev = m_scr[...]
    m_new = jnp.maximum(m_prev, jnp.max(s, axis=-1, keepdims=True))
    alpha = jnp.exp(m_prev - m_new)
    p = jnp.exp(s - m_new)
    l_scr[...] = alpha * l_scr[...] + jnp.sum(p, axis=-1, keepdims=True)
    pb = p.astype(BF16)
    pv = jnp.zeros(acc_scr.shape, F32)
    for u, v_ref in enumerate(v_refs):
        spread = _dot(pb[:, u * page:(u + 1) * page], spread_ref[...]) * headmask_ref[...]
        pv = pv + _dot(spread.astype(BF16), v_ref[...].astype(BF16))
    acc_scr[...] = alpha * acc_scr[...] + pv
    m_scr[...] = m_new

    @pl.when(step == pl.num_programs(1) - 1)
    def _():
        lam = _lambda(lam_ref, lam_init)
        lane = lax.broadcasted_iota(I32, (rows, LANES), 1)
        q_id = lax.broadcasted_iota(I32, (rows, LANES), 0) % SUBLANES
        qf = jnp.concatenate([q_ref[0, :, h * hw:(h + 1) * hw] for h in range(n_heads)],
                             axis=0).astype(F32)

        def per_head(ref, t):
            return jnp.concatenate(
                [jnp.broadcast_to(ref[0, t:t + 1, h * hw:(h + 1) * hw], (grp, hw))
                 for h in range(n_heads)], axis=0)

        sn = jnp.full((rows, LANES), NEG, F32)
        for t in range(dec_seq):
            st = jnp.sum(qf * per_head(kn_ref, t), axis=-1, keepdims=True) * scale
            sn = jnp.where((lane == t) & (t <= q_id), st, sn)
        m_old = m_scr[...]
        m_fin = jnp.maximum(m_old, jnp.max(sn, axis=-1, keepdims=True))
        a_fin = jnp.exp(m_old - m_fin)
        pn = jnp.exp(sn - m_fin)
        l_fin = a_fin * l_scr[...] + jnp.sum(pn, axis=-1, keepdims=True)
        acc = a_fin * acc_scr[...]
        for t in range(dec_seq):
            pt = jnp.sum(jnp.where(lane == t, pn, 0.0), axis=-1, keepdims=True)
            acc = acc + pt * per_head(vn_ref, t)
        o = acc / l_fin
        for h in range(n_heads):
            oh = o[h * grp:h * grp + SUBLANES] - lam * o[h * grp + SUBLANES:(h + 1) * grp]
            o_ref[0, :, h * hw:(h + 1) * hw] = (_rms(oh, sub_ref[...]) * (1.0 - lam_init)).astype(BF16)


def _decode_queries(q_rows, dec_batch, dec_seq):
    d = q_rows.shape[1]
    qs = q_rows.reshape(dec_batch, dec_seq, d // (2 * HEAD_DIM), 2, HEAD_DIM)
    qs = jnp.pad(qs, ((0, 0), (0, SUBLANES - dec_seq), (0, 0), (0, 0), (0, 0)))
    q16 = jnp.einsum("bqhcd,ce->bcqhed", qs, jnp.eye(2, dtype=qs.dtype))
    return q16.reshape(dec_batch, 2 * SUBLANES, d)


def _decode_attn(q16, kn8, vn8, cache_k, cache_v, page_table, page, b_lam, subln, dec_seq, lam_init,
                 pages_per_step=4):
    dec_batch, _, width = q16.shape
    n_pages = page_table.shape[1]
    hw = 2 * HEAD_DIM
    n_heads = width // hw
    pps = pages_per_step if n_pages % pages_per_step == 0 else 1
    grp = 2 * SUBLANES
    rows = n_heads * grp
    col = jnp.arange(page * n_heads)
    spread = (col[None, :] // n_heads == jnp.arange(page)[:, None]).astype(BF16)
    headmask = (col[None, :] % n_heads == jnp.arange(rows)[:, None] // grp).astype(F32)

    def k_spec(u):
        return pl.BlockSpec((page * 2 * n_heads, HEAD_DIM),
                            lambda s, p, pt: (pt[s * n_pages + p * pps + u], 0))

    def v_spec(u):
        return pl.BlockSpec((page * n_heads, hw),
                            lambda s, p, pt: (pt[s * n_pages + p * pps + u], 0))

    def seq_spec(r):
        return pl.BlockSpec((1, r, width), lambda s, p, pt: (s, 0, 0))

    const = lambda a: pl.BlockSpec(a.shape, lambda s, p, pt: (0, 0))
    grid_spec = pltpu.PrefetchScalarGridSpec(
        num_scalar_prefetch=1,
        grid=(dec_batch, n_pages // pps),
        in_specs=[
            const(b_lam), const(subln), const(spread), const(headmask),
            seq_spec(grp), seq_spec(SUBLANES), seq_spec(SUBLANES),
        ] + [k_spec(u) for u in range(pps)] + [v_spec(u) for u in range(pps)],
        out_specs=seq_spec(SUBLANES),
        scratch_shapes=[
            pltpu.VMEM((rows, 1), F32),
            pltpu.VMEM((rows, 1), F32),
            pltpu.VMEM((rows, hw), F32),
        ],
    )
    return pl.pallas_call(
        functools.partial(_decode_attn_kernel, pages_per_step=pps, n_heads=n_heads, dec_seq=dec_seq,
                          lam_init=lam_init, scale=HEAD_DIM ** -0.5),
        grid_spec=grid_spec,
        out_shape=jax.ShapeDtypeStruct((dec_batch, SUBLANES, width), BF16),
        compiler_params=_params(("parallel", "arbitrary")),
        name="diff_attn_decode",
    )(page_table.reshape(-1), b_lam, subln, spread, headmask, q16, kn8, vn8,
      *([cache_k] * pps), *([cache_v] * pps))


def _wo_router_kernel(o_ref, x_ref, g_ref, wo_ref, wr_ref, br_ref, tri_ref, cnt0_ref,
                      x3_ref, hm_ref, meta_ref, cnt_ref, carry_scr, *, n_experts):
    i = pl.program_id(0)

    @pl.when(i == 0)
    def _():
        carry_scr[...] = cnt0_ref[...]

    x3 = x_ref[...] + _rms(_dot(o_ref[...], wo_ref[...]), g_ref[1:2, :])
    x3_ref[...] = x3
    hm = _rms(x3, g_ref[2:3, :])
    hm_ref[...] = hm
    logits = jnp.dot(hm, wr_ref[...], preferred_element_type=F32,
                     precision=lax.Precision.HIGHEST) + br_ref[...]
    lane = lax.broadcasted_iota(I32, logits.shape, 1)
    logits = jnp.where(lane < n_experts, logits, -jnp.inf)
    m1 = jnp.max(logits, axis=-1, keepdims=True)
    i1 = jnp.min(jnp.where(logits == m1, lane, LANES), axis=-1, keepdims=True)
    rest = jnp.where(lane == i1, -jnp.inf, logits)
    m2 = jnp.max(rest, axis=-1, keepdims=True)
    i2 = jnp.min(jnp.where(rest == m2, lane, LANES), axis=-1, keepdims=True)
    e2 = jnp.exp(m2 - m1)
    g1 = 1.0 / (1.0 + e2)
    g2 = e2 / (1.0 + e2)
    oh1 = (lane == i1).astype(F32)
    oh2 = (lane == i2).astype(F32)
    tri = tri_ref[...]
    before1 = _dot(tri, oh1.astype(BF16))
    before2 = _dot(tri, oh2.astype(BF16))
    tot1 = jnp.sum(oh1, axis=0, keepdims=True)
    tot2 = jnp.sum(oh2, axis=0, keepdims=True)
    carry = carry_scr[...]
    r1 = jnp.sum(oh1 * (carry + before1), axis=-1, keepdims=True)
    r2 = jnp.sum(oh2 * (carry + tot1 + before2), axis=-1, keepdims=True)
    carry = carry + tot1 + tot2
    carry_scr[...] = carry
    cnt_ref[...] = carry
    meta = jnp.zeros(logits.shape, F32)
    for col, val in enumerate((i1.astype(F32), i2.astype(F32), g1, g2, r1, r2)):
        meta = jnp.where(lane == col, val, meta)
    meta_ref[...] = meta


def _wo_router(o, x, g4, w_o, w_r, b_r, cnt0, n_experts, tm_pref=TILE):
    n, d = x.shape
    tm = _tile(n, tm_pref)
    row = lambda i: (i, 0)
    tri = jnp.tril(jnp.ones((tm, tm), BF16), k=-1)
    return pl.pallas_call(
        functools.partial(_wo_router_kernel, n_experts=n_experts),
        grid=(n // tm,),
        in_specs=[
            pl.BlockSpec((tm, d), row),
            pl.BlockSpec((tm, d), row),
            _resident((4, d)),
            _resident(w_o.shape),
            _resident(w_r.shape),
            _resident(b_r.shape),
            _resident((tm, tm)),
            _resident((1, LANES)),
        ],
        out_specs=[
            pl.BlockSpec((tm, d), row),
            pl.BlockSpec((tm, d), row),
            pl.BlockSpec((tm, LANES), row),
            pl.BlockSpec((1, LANES), lambda i: (0, 0)),
        ],
        out_shape=[
            jax.ShapeDtypeStruct((n, d), F32),
            jax.ShapeDtypeStruct((n, d), F32),
            jax.ShapeDtypeStruct((n, LANES), F32),
            jax.ShapeDtypeStruct((1, LANES), F32),
        ],
        scratch_shapes=[pltpu.VMEM((1, LANES), F32)],
        compiler_params=_params(("arbitrary",)),
        name="wo_router",
    )(o, x, g4, w_o, w_r, b_r, tri, cnt0)


def _row_copy(src_ref, src_row, dst_ref, dst_row, sem):
    return pltpu.make_async_copy(src_ref.at[pl.ds(src_row, 1), :], dst_ref.at[pl.ds(dst_row, 1), :], sem)


def _dispatch_kernel(pos_ref, hm_ref, init_ref, xs_ref, sem):
    del init_ref
    tm = hm_ref.shape[0]
    base = pl.program_id(0) * tm * TOP_K

    def start(r, carry):
        for k in range(TOP_K):
            _row_copy(hm_ref, r, xs_ref, pos_ref[base + r * TOP_K + k], sem).start()
        return carry

    def wait(r, carry):
        for k in range(TOP_K):
            _row_copy(hm_ref, r, xs_ref, pos_ref[base + r * TOP_K + k], sem).wait()
        return carry

    lax.fori_loop(0, tm, start, 0)
    lax.fori_loop(0, tm, wait, 0)


def _dispatch(hm, pos_flat, xs, tm_pref=TILE):
    n, d = hm.shape
    tm = _tile(n, tm_pref)
    grid_spec = pltpu.PrefetchScalarGridSpec(
        num_scalar_prefetch=1,
        grid=(n // tm,),
        in_specs=[
            pl.BlockSpec((tm, d), lambda i, pos: (i, 0)),
            pl.BlockSpec(memory_space=pl.ANY),
        ],
        out_specs=pl.BlockSpec(memory_space=pl.ANY),
        scratch_shapes=[pltpu.SemaphoreType.DMA(())],
    )
    return pl.pallas_call(
        _dispatch_kernel,
        grid_spec=grid_spec,
        out_shape=jax.ShapeDtypeStruct(xs.shape, F32),
        input_output_aliases={2: 0},
        compiler_params=_params(("arbitrary",)),
        name="moe_dispatch",
    )(pos_flat, hm, xs)


def _combine_kernel(pos_ref, x_ref, g_ref, meta_ref, ys_ref, out_ref, a_scr, b_scr, sem):
    tm = x_ref.shape[0]
    base = pl.program_id(0) * tm * TOP_K
    bufs = (a_scr, b_scr)

    def start(r, carry):
        for k in range(TOP_K):
            _row_copy(ys_ref, pos_ref[base + r * TOP_K + k], bufs[k], r, sem).start()
        return carry

    def wait(r, carry):
        for k in range(TOP_K):
            _row_copy(ys_ref, pos_ref[base + r * TOP_K + k], bufs[k], r, sem).wait()
        return carry

    lax.fori_loop(0, tm, start, 0)
    lax.fori_loop(0, tm, wait, 0)
    meta = meta_ref[...]
    f = meta[:, 2:3] * a_scr[...] + meta[:, 3:4] * b_scr[...]
    out_ref[...] = x_ref[...] + _rms(f, g_ref[3:4, :])


def _combine(x, g4, meta, ys, pos_flat, tm_pref=TILE):
    n, d = x.shape
    tm = _tile(n, tm_pref)
    grid_spec = pltpu.PrefetchScalarGridSpec(
        num_scalar_prefetch=1,
        grid=(n // tm,),
        in_specs=[
            pl.BlockSpec((tm, d), lambda i, pos: (i, 0)),
            pl.BlockSpec((4, d), lambda i, pos: (0, 0)),
            pl.BlockSpec((tm, LANES), lambda i, pos: (i, 0)),
            pl.BlockSpec(memory_space=pl.ANY),
        ],
        out_specs=pl.BlockSpec((tm, d), lambda i, pos: (i, 0)),
        scratch_shapes=[pltpu.VMEM((tm, d), F32), pltpu.VMEM((tm, d), F32),
                        pltpu.SemaphoreType.DMA(())],
    )
    return pl.pallas_call(
        _combine_kernel,
        grid_spec=grid_spec,
        out_shape=jax.ShapeDtypeStruct((n, d), F32),
        compiler_params=_params(("arbitrary",)),
        name="moe_combine",
    )(pos_flat, x, g4, meta, ys)


def _rope_tables(pos):
    inv = jnp.exp(-math.log(ROPE_THETA) * jnp.arange(0, HEAD_DIM, 2, dtype=F32) / HEAD_DIM)
    ang = pos.astype(F32)[:, None] * inv[None, :]
    ang = jnp.concatenate([ang, ang], axis=-1)
    sign = jnp.where(jnp.arange(HEAD_DIM) < HEAD_DIM // 2, -1.0, 1.0).astype(F32)
    return jnp.cos(ang), jnp.sin(ang) * sign


def _mix_tables(w_s, b_s, dec_seq):
    causal = jnp.tril(jnp.ones((CHUNK, CHUNK), dtype=bool))
    w_prompt = jnp.where(causal[None], w_s, 0)
    per = CHUNK // dec_seq
    w_small = w_prompt[:, :dec_seq, :dec_seq]
    eye = jnp.eye(per, dtype=w_s.dtype)
    w_sample = jnp.einsum("ab,gts->gatbs", eye, w_small).reshape(N_GROUPS, CHUNK, CHUNK)
    gd = LANES
    b_prompt = jnp.repeat(jnp.transpose(b_s), gd, axis=1)
    b_sample = jnp.repeat(jnp.tile(jnp.transpose(b_s)[:dec_seq], (per, 1)), gd, axis=1)
    return (w_prompt.astype(BF16), b_prompt), (w_sample.astype(BF16), b_sample)


def kernel(x_prompt, x_sample, cache_k, cache_v, page_table, norm_g, a_w_in, a_ln_g, a_ln_b, a_w_s,
           a_b_s, a_w_out, f_w_gu, f_w_down, b_w_qkv, b_lam, b_subln, b_w_o, e_w_r, e_b_r, e_w_gu,
           e_w_down):
    batch, seq, d = x_prompt.shape
    dec_batch, dec_seq, _ = x_sample.shape
    assert norm_g.shape[0] == 2 and a_w_in.shape[0] == 1 and b_w_qkv.shape[0] == 1
    n_p, n_s = batch * seq, dec_batch * dec_seq
    n = n_p + n_s
    n_experts = e_w_r.shape[-1]
    n_heads = d // (2 * HEAD_DIM)
    page = cache_k.shape[2]
    past = page_table.shape[1] * page
    assert CHUNK % dec_seq == 0 and dec_seq <= SUBLANES
    tm_e = TILE
    xp, xs_ = x_prompt.reshape(n_p, d), x_sample.reshape(n_s, d)

    mix_p, mix_s = _mix_tables(a_w_s[0], a_b_s[0], dec_seq)
    w_in, w_out = a_w_in[0].astype(BF16), a_w_out[0].astype(BF16)
    w_gu, w_down = f_w_gu[0].astype(BF16), f_w_down[0].astype(BF16)

    def layer0(x, mix):
        x, v_rows = _gmlp_layer(x, norm_g[0], w_in, a_ln_g, a_ln_b, mix[0], mix[1], w_out)
        return _swiglu_layer(x, norm_g[0], w_gu, w_down), v_rows

    xp, _ = layer0(xp, mix_p)
    xs_, v_rows = layer0(xs_, mix_s)

    lam_init = 0.8 - 0.6 * math.exp(-0.3 * 1)
    w_qkv = b_w_qkv[0].astype(BF16)
    tab_rows = _tile(n_s, TILE)
    assert tab_rows % dec_seq == 0
    q_p, kb_p, vb_p, k_p, v_p = _qkv_layer(xp, norm_g[1], *_rope_tables(jnp.arange(seq)), w_qkv)
    q_s, kb_s, vb_s, k_s, v_s = _qkv_layer(
        xs_, norm_g[1], *_rope_tables(past + jnp.arange(tab_rows) % dec_seq), w_qkv)
    o_p = _prompt_attn(q_p, kb_p, vb_p, b_lam[0], b_subln, batch, seq, lam_init)
    pad8 = lambda a: jnp.pad(a.reshape(dec_batch, dec_seq, d).astype(F32),
                             ((0, 0), (0, SUBLANES - dec_seq), (0, 0)))
    o_s = _decode_attn(_decode_queries(q_s, dec_batch, dec_seq), pad8(kb_s), pad8(vb_s),
                       cache_k.reshape(-1, HEAD_DIM), cache_v.reshape(-1, 2 * HEAD_DIM),
                       page_table, page, b_lam[0], b_subln, dec_seq, lam_init)
    o_s = o_s[:, :dec_seq].reshape(n_s, d)

    w_o = b_w_o[0].astype(BF16)
    w_r = jnp.pad(e_w_r[0], ((0, 0), (0, LANES - n_experts)))
    b_r = jnp.pad(e_b_r, ((0, 0), (0, LANES - n_experts)))
    cnt = jnp.zeros((1, LANES), F32)
    xp, hm_p, meta_p, cnt = _wo_router(o_p, xp, norm_g[1], w_o, w_r, b_r, cnt, n_experts)
    xs_, hm_s, meta_s, cnt = _wo_router(o_s, xs_, norm_g[1], w_o, w_r, b_r, cnt, n_experts)
    counts = cnt[0, :n_experts].astype(I32)
    tiles_per = (counts + tm_e - 1) // tm_e
    tile_end = jnp.cumsum(tiles_per)
    starts = (tile_end - tiles_per) * tm_e

    def slots(meta):
        expert = meta[:, 0:TOP_K].astype(I32)
        rank = meta[:, 4:4 + TOP_K].astype(I32)
        return (starts[expert] + rank).reshape(-1)

    pos_p, pos_s = slots(meta_p), slots(meta_s)
    n_tiles = (TOP_K * (n_p + n_s) + n_experts * (tm_e - 1)) // tm_e
    n_active = tile_end[-1:]
    tile_id = jnp.minimum(jnp.arange(n_tiles), n_active[0] - 1)
    tile_expert = jnp.sum(tile_id[:, None] >= tile_end[None, :], axis=1).astype(I32)
    xs = jnp.zeros((n_tiles * tm_e, d), F32)
    xs = _dispatch(hm_p, pos_p, xs)
    xs = _dispatch(hm_s, pos_s, xs)
    ys = _experts(xs, tile_expert, n_active.astype(I32), e_w_gu[0].astype(BF16),
                  e_w_down[0].astype(BF16), tm_e)
    yp = _combine(xp, norm_g[1], meta_p, ys, pos_p)
    ysm = _combine(xs_, norm_g[1], meta_s, ys, pos_s)

    return (
        yp.reshape(batch, seq, d),
        ysm.reshape(dec_batch, dec_seq, d),
        k_p.reshape(1, batch, seq, n_heads, 2, HEAD_DIM),
        v_p.reshape(1, batch, seq, n_heads, 2 * HEAD_DIM),
        k_s.reshape(1, dec_batch, dec_seq, n_heads, 2, HEAD_DIM),
        v_s.reshape(1, dec_batch, dec_seq, n_heads, 2 * HEAD_DIM),
        v_rows.reshape(1, dec_batch, dec_seq, -1),
    )
```

```python
import functools
import math

import jax
import jax.numpy as jnp
from jax import lax
from jax.experimental import pallas as pl
from jax.experimental.pallas import tpu as pltpu

F32 = jnp.float32
BF16 = jnp.bfloat16
I32 = jnp.int32

EPS = 1e-6
CHUNK = 128
N_GROUPS = 16
HEAD_DIM = 128
ROPE_THETA = 10000.0
TOP_K = 2
NEG = -1e30
LANES = 128
SUBLANES = 8
VMEM_LIMIT = 56 * 1024 * 1024
TILE = 512


def _dot(a, b):
    return jnp.dot(a, b, preferred_element_type=F32)


def _dot_nt(a, b):
    return lax.dot_general(a, b, (((1,), (1,)), ((), ())), preferred_element_type=F32)


def _rms(x, g):
    return x * lax.rsqrt(jnp.mean(x * x, axis=-1, keepdims=True) + EPS) * g


def _tile(n, pref):
    t = min(n, pref)
    while n % t:
        t -= SUBLANES
    assert t > 0 and t % SUBLANES == 0, (n, pref)
    return t


def _params(sem, vmem=VMEM_LIMIT):
    return pltpu.CompilerParams(dimension_semantics=sem, vmem_limit_bytes=vmem)


def _resident(shape):
    return pl.BlockSpec(shape, lambda *_: (0,) * len(shape), pipeline_mode=pl.Buffered(1))


def _gmlp_kernel(x_ref, g_ref, win_ref, lng_ref, lnb_ref, wmix_ref, bias_ref, wout_ref,
                 out_ref, v_ref, h_scr, u_scr, vb_scr, y_scr, *, col_w):
    tm, d = x_ref.shape
    d_inner = u_scr.shape[1]
    x = x_ref[...]
    h_scr[...] = _rms(x, g_ref[0:1, :]).astype(BF16)
    n_col = d_inner // col_w
    s1 = jnp.zeros((tm, 1), F32)
    for c in range(n_col):
        cs = slice(c * col_w, (c + 1) * col_w)
        u_scr[:, cs] = jax.nn.gelu(_dot(h_scr[...], win_ref[:, cs]))
        v = jax.nn.gelu(_dot(h_scr[...], win_ref[:, d_inner + c * col_w:d_inner + (c + 1) * col_w]))
        v_ref[:, cs] = v
        s1 = s1 + jnp.sum(v, axis=-1, keepdims=True)
    mu = s1 / d_inner
    s2 = jnp.zeros((tm, 1), F32)
    for c in range(n_col):
        cs = slice(c * col_w, (c + 1) * col_w)
        dv = v_ref[:, cs] - mu
        s2 = s2 + jnp.sum(dv * dv, axis=-1, keepdims=True)
    rstd = lax.rsqrt(s2 / d_inner + EPS)
    for c in range(n_col):
        cs = slice(c * col_w, (c + 1) * col_w)
        vn = (v_ref[:, cs] - mu) * rstd * lng_ref[:, cs] + lnb_ref[:, cs]
        v_ref[:, cs] = vn
        vb_scr[:, cs] = vn.astype(BF16)
    gd = d_inner // N_GROUPS
    for r in range(tm // CHUNK):
        rs = slice(r * CHUNK, (r + 1) * CHUNK)
        for g in range(N_GROUPS):
            gs = slice(g * gd, (g + 1) * gd)
            z = _dot(wmix_ref[g], vb_scr[rs, gs]) + bias_ref[:, gs]
            y_scr[rs, gs] = (u_scr[rs, gs] * z).astype(BF16)
    m = _dot(y_scr[...], wout_ref[...])
    out_ref[...] = x + _rms(m, g_ref[1:2, :])


def _gmlp_layer(x, g4, w_in, ln_g, ln_b, wmix, bias, w_out):
    n, d = x.shape
    d_inner = w_out.shape[0]
    tm = _tile(n, 2 * CHUNK)
    assert tm % CHUNK == 0 and d_inner // N_GROUPS == LANES
    row = lambda i: (i, 0)
    return pl.pallas_call(
        functools.partial(_gmlp_kernel, col_w=_tile(d_inner, TILE)),
        grid=(n // tm,),
        in_specs=[
            pl.BlockSpec((tm, d), row),
            _resident((4, d)),
            _resident(w_in.shape),
            _resident((1, d_inner)),
            _resident((1, d_inner)),
            _resident(wmix.shape),
            _resident(bias.shape),
            _resident(w_out.shape),
        ],
        out_specs=[pl.BlockSpec((tm, d), row), pl.BlockSpec((tm, d_inner), row)],
        out_shape=[jax.ShapeDtypeStruct((n, d), F32), jax.ShapeDtypeStruct((n, d_inner), F32)],
        scratch_shapes=[
            pltpu.VMEM((tm, d), BF16),
            pltpu.VMEM((tm, d_inner), F32),
            pltpu.VMEM((tm, d_inner), BF16),
            pltpu.VMEM((tm, d_inner), BF16),
        ],
        compiler_params=_params(("parallel",)),
        name="gmlp_layer",
    )(x, g4, w_in, ln_g, ln_b, wmix, bias, w_out)


def _swiglu_step(h, wg_ref, wu_ref, wd_ref, acc_ref):
    g = _dot(h, wg_ref[...].astype(BF16))
    u = _dot(h, wu_ref[...].astype(BF16))
    a = (jax.nn.silu(g) * u).astype(BF16)
    acc_ref[...] += _dot(a, wd_ref[...].astype(BF16))


def _swiglu_kernel(x_ref, g_ref, wg_ref, wu_ref, wd_ref, out_ref, h_scr, acc_scr):
    f = pl.program_id(1)

    @pl.when(f == 0)
    def _():
        h_scr[...] = _rms(x_ref[...], g_ref[2:3, :]).astype(BF16)
        acc_scr[...] = jnp.zeros_like(acc_scr)

    _swiglu_step(h_scr[...], wg_ref, wu_ref, wd_ref, acc_scr)

    @pl.when(f == pl.num_programs(1) - 1)
    def _():
        out_ref[...] = x_ref[...] + _rms(acc_scr[...], g_ref[3:4, :])


def _swiglu_layer(x, g4, w_gu, w_down, tm_pref=TILE, tf_pref=TILE):
    n, d = x.shape
    d_ff = w_down.shape[0]
    tm = _tile(n, tm_pref)
    tf = _tile(d_ff, tf_pref)
    nf = d_ff // tf
    return pl.pallas_call(
        _swiglu_kernel,
        grid=(n // tm, nf),
        in_specs=[
            pl.BlockSpec((tm, d), lambda i, f: (i, 0)),
            pl.BlockSpec((4, d), lambda i, f: (0, 0)),
            pl.BlockSpec((d, tf), lambda i, f: (0, f)),
            pl.BlockSpec((d, tf), lambda i, f: (0, f + nf)),
            pl.BlockSpec((tf, d), lambda i, f: (f, 0)),
        ],
        out_specs=pl.BlockSpec((tm, d), lambda i, f: (i, 0)),
        out_shape=jax.ShapeDtypeStruct((n, d), F32),
        scratch_shapes=[pltpu.VMEM((tm, d), BF16), pltpu.VMEM((tm, d), F32)],
        compiler_params=_params(("parallel", "arbitrary")),
        name="swiglu_dense",
    )(x, g4, w_gu, w_gu, w_down)


def _experts_kernel(te_ref, na_ref, xs_ref, wg_ref, wu_ref, wd_ref, ys_ref, h_scr):
    i = pl.program_id(0)
    f = pl.program_id(1)
    active = i < na_ref[0]

    @pl.when(f == 0)
    def _():
        h_scr[...] = xs_ref[...].astype(BF16)
        ys_ref[...] = jnp.zeros_like(ys_ref)

    @pl.when(active)
    def _():
        _swiglu_step(h_scr[...], wg_ref.at[0], wu_ref.at[0], wd_ref.at[0], ys_ref)


def _experts(xs, tile_expert, n_active, w_gu, w_down, tm, tf_pref=TILE):
    rows, d = xs.shape
    d_ff = w_down.shape[1]
    tf = _tile(d_ff, tf_pref)
    nf = d_ff // tf

    def fidx(i, f, na):
        return jnp.where(i < na[0], f, nf - 1)

    grid_spec = pltpu.PrefetchScalarGridSpec(
        num_scalar_prefetch=2,
        grid=(rows // tm, nf),
        in_specs=[
            pl.BlockSpec((tm, d), lambda i, f, te, na: (i, 0)),
            pl.BlockSpec((1, d, tf), lambda i, f, te, na: (te[i], 0, fidx(i, f, na))),
            pl.BlockSpec((1, d, tf), lambda i, f, te, na: (te[i], 0, fidx(i, f, na) + nf)),
            pl.BlockSpec((1, tf, d), lambda i, f, te, na: (te[i], fidx(i, f, na), 0)),
        ],
        out_specs=pl.BlockSpec((tm, d), lambda i, f, te, na: (i, 0)),
        scratch_shapes=[pltpu.VMEM((tm, d), BF16)],
    )
    return pl.pallas_call(
        _experts_kernel,
        grid_spec=grid_spec,
        out_shape=jax.ShapeDtypeStruct((rows, d), F32),
        compiler_params=_params(("arbitrary", "arbitrary")),
        name="swiglu_experts",
    )(tile_expert, n_active, xs, w_gu, w_gu, w_down)


def _rope(x, cos, sin_signed):
    parts = []
    for s in range(x.shape[1] // HEAD_DIM):
        xs = x[:, s * HEAD_DIM:(s + 1) * HEAD_DIM]
        parts.append(xs * cos + pltpu.roll(xs, HEAD_DIM // 2, axis=1) * sin_signed)
    return jnp.concatenate(parts, axis=1)


def _qkv_kernel(x_ref, g_ref, cos_ref, sin_ref, wq_ref, wk_ref, wv_ref,
                q_ref, kb_ref, vb_ref, k_ref, v_ref, h_scr):
    j = pl.program_id(1)

    @pl.when(j == 0)
    def _():
        h_scr[...] = _rms(x_ref[...], g_ref[0:1, :]).astype(BF16)

    h = h_scr[...]
    cos = cos_ref[...]
    sin = sin_ref[...]
    tm, tn = q_ref.shape
    n_heads, hw = v_ref.shape[1], v_ref.shape[2]
    q_ref[...] = _rope(_dot(h, wq_ref[...]), cos, sin).astype(BF16)
    k = _rope(_dot(h, wk_ref[...]), cos, sin)
    v = _dot(h, wv_ref[...])
    kb_ref[...] = k.astype(BF16)
    vb_ref[...] = v.astype(BF16)
    slabs = 2 * n_heads
    for jj in range(slabs * HEAD_DIM // tn):
        @pl.when(j == jj)
        def _():
            for s in range(tn // HEAD_DIM):
                k_ref[pl.ds(jj * (tn // HEAD_DIM) + s, tm, stride=slabs), :] = (
                    k[:, s * HEAD_DIM:(s + 1) * HEAD_DIM])
            for s in range(tn // hw):
                v_ref[:, jj * (tn // hw) + s, :] = v[:, s * hw:(s + 1) * hw]


def _qkv_layer(x, g4, cos_tab, sin_tab, w_qkv, tm_pref=TILE, tn_pref=TILE):
    n, d = x.shape
    width = w_qkv.shape[1] // 3
    tm = _tile(math.gcd(n, cos_tab.shape[0]), tm_pref)
    tn = _tile(width, tn_pref)
    nj = width // tn
    period = cos_tab.shape[0] // tm
    slabs = width // HEAD_DIM
    assert tn % (2 * HEAD_DIM) == 0
    tab = lambda i, j: (i % period, 0)
    blk = lambda i, j: (i, j)
    return pl.pallas_call(
        _qkv_kernel,
        grid=(n // tm, nj),
        in_specs=[
            pl.BlockSpec((tm, d), lambda i, j: (i, 0)),
            pl.BlockSpec((4, d), lambda i, j: (0, 0)),
            pl.BlockSpec((tm, HEAD_DIM), tab),
            pl.BlockSpec((tm, HEAD_DIM), tab),
            pl.BlockSpec((d, tn), lambda i, j: (0, j)),
            pl.BlockSpec((d, tn), lambda i, j: (0, j + nj)),
            pl.BlockSpec((d, tn), lambda i, j: (0, j + 2 * nj)),
        ],
        out_specs=[pl.BlockSpec((tm, tn), blk)] * 3 + [
            pl.BlockSpec((tm * slabs, HEAD_DIM), lambda i, j: (i, 0)),
            pl.BlockSpec((tm, slabs // 2, 2 * HEAD_DIM), lambda i, j: (i, 0, 0)),
        ],
        out_shape=[jax.ShapeDtypeStruct((n, width), BF16)] * 3 + [
            jax.ShapeDtypeStruct((n * slabs, HEAD_DIM), F32),
            jax.ShapeDtypeStruct((n, slabs // 2, 2 * HEAD_DIM), F32),
        ],
        scratch_shapes=[pltpu.VMEM((tm, d), BF16)],
        compiler_params=_params(("parallel", "arbitrary")),
        name="qkv_rope",
    )(x, g4, cos_tab, sin_tab, w_qkv, w_qkv, w_qkv)


def _lambda(lam_ref, lam_init):
    lp = lam_ref[...]
    a = jnp.sum(lp[0:1, :] * lp[1:2, :], axis=-1, keepdims=True)
    b = jnp.sum(lp[2:3, :] * lp[3:4, :], axis=-1, keepdims=True)
    return jnp.exp(a) - jnp.exp(b) + lam_init


def _prompt_attn_kernel(lam_ref, sub_ref, q_ref, k_ref, v_ref, o_ref, *, lam_init, scale, n_q):
    qi = pl.program_id(2)
    tq = q_ref.shape[0]

    def tile_body(t):
        w0 = t * tq
        outs = []
        for c in range(2):
            cs = slice(c * HEAD_DIM, (c + 1) * HEAD_DIM)
            q = q_ref[:, cs]
            sd = _dot_nt(q, k_ref[w0:w0 + tq, cs]) * scale
            r_id = lax.broadcasted_iota(I32, sd.shape, 0)
            c_id = lax.broadcasted_iota(I32, sd.shape, 1)
            sd = jnp.where(c_id <= r_id, sd, NEG)
            m = jnp.max(sd, axis=-1, keepdims=True)
            if t > 0:
                sm = _dot_nt(q, k_ref[0:w0, cs]) * scale
                m = jnp.maximum(m, jnp.max(sm, axis=-1, keepdims=True))
            pd = jnp.exp(sd - m)
            l = jnp.sum(pd, axis=-1, keepdims=True)
            acc = _dot(pd.astype(BF16), v_ref[w0:w0 + tq, :])
            if t > 0:
                pm = jnp.exp(sm - m)
                l = l + jnp.sum(pm, axis=-1, keepdims=True)
                acc = acc + _dot(pm.astype(BF16), v_ref[0:w0, :])
            outs.append(acc / l)
        lam = _lambda(lam_ref, lam_init)
        o = outs[0] - lam * outs[1]
        o_ref[...] = (_rms(o, sub_ref[...]) * (1.0 - lam_init)).astype(BF16)

    for t in range(n_q):
        pl.when(qi == t)(functools.partial(tile_body, t))


def _prompt_attn(q, k, v, b_lam, subln, batch, seq, lam_init, tq_pref=TILE):
    n_heads = q.shape[1] // (2 * HEAD_DIM)
    hw = 2 * HEAD_DIM
    tq = _tile(seq, tq_pref)
    nq = seq // tq
    kv_spec = pl.BlockSpec((seq, hw), lambda b, h, i: (b, h))
    return pl.pallas_call(
        functools.partial(_prompt_attn_kernel, lam_init=lam_init, scale=HEAD_DIM ** -0.5, n_q=nq),
        grid=(batch, n_heads, nq),
        in_specs=[
            pl.BlockSpec((4, HEAD_DIM), lambda b, h, i: (0, 0)),
            pl.BlockSpec((1, hw), lambda b, h, i: (0, 0)),
            pl.BlockSpec((tq, hw), lambda b, h, i: (b * nq + i, h)),
            kv_spec,
            kv_spec,
        ],
        out_specs=pl.BlockSpec((tq, hw), lambda b, h, i: (b * nq + i, h)),
        out_shape=jax.ShapeDtypeStruct((batch * seq, n_heads * hw), BF16),
        compiler_params=_params(("parallel", "parallel", "arbitrary")),
        name="diff_attn_prompt",
    )(b_lam, subln, q, k, v)


def _decode_attn_kernel(pt_ref, lam_ref, sub_ref, spread_ref, headmask_ref, q_ref, kn_ref, vn_ref,
                        *rest, pages_per_step, n_heads, dec_seq, lam_init, scale):
    k_refs = rest[:pages_per_step]
    v_refs = rest[pages_per_step:2 * pages_per_step]
    o_ref, m_scr, l_scr, acc_scr = rest[2 * pages_per_step:]
    del pt_ref
    step = pl.program_id(1)
    hw = 2 * HEAD_DIM
    grp = 2 * SUBLANES
    slabs = 2 * n_heads
    page = k_refs[0].shape[0] // slabs
    rows = n_heads * grp

    @pl.when(step == 0)
    def _():
        m_scr[...] = jnp.full_like(m_scr, NEG)
        l_scr[...] = jnp.zeros_like(l_scr)
        acc_scr[...] = jnp.zeros_like(acc_scr)

    def head_keys(k_ref, h):
        return jnp.concatenate([k_ref[pl.ds(2 * h + c, page, stride=slabs), :] for c in range(2)],
                               axis=1).astype(BF16)

    s = jnp.concatenate(
        [jnp.concatenate([_dot_nt(q_ref[0, :, h * hw:(h + 1) * hw], head_keys(k_ref, h))
                          for k_ref in k_refs], axis=1)
         for h in range(n_heads)], axis=0) * scale
    m_prev = m_scr[...]
    m_new = jnp.maximum(m_prev, jnp.max(s, axis=-1, keepdims=True))
    alpha = jnp.exp(m_prev - m_new)
    p = jnp.exp(s - m_new)
    l_scr[...] = alpha * l_scr[...] + jnp.sum(p, axis=-1, keepdims=True)
    pb = p.astype(BF16)
    pv = jnp.zeros(acc_scr.shape, F32)
    for u, v_ref in enumerate(v_refs):
        spread = _dot(pb[:, u * page:(u + 1) * page], spread_ref[...]) * headmask_ref[...]
        pv = pv + _dot(spread.astype(BF16), v_ref[...].astype(BF16))
    acc_scr[...] = alpha * acc_scr[...] + pv
    m_scr[...] = m_new

    @pl.when(step == pl.num_programs(1) - 1)
    def _():
        lam = _lambda(lam_ref, lam_init)
        lane = lax.broadcasted_iota(I32, (rows, LANES), 1)
        q_id = lax.broadcasted_iota(I32, (rows, LANES), 0) % SUBLANES
        qf = jnp.concatenate([q_ref[0, :, h * hw:(h + 1) * hw] for h in range(n_heads)],
                             axis=0).astype(F32)

        def per_head(ref, t):
            return jnp.concatenate(
                [jnp.broadcast_to(ref[0, t:t + 1, h * hw:(h + 1) * hw], (grp, hw))
                 for h in range(n_heads)], axis=0)

        sn = jnp.full((rows, LANES), NEG, F32)
        for t in range(dec_seq):
            st = jnp.sum(qf * per_head(kn_ref, t), axis=-1, keepdims=True) * scale
            sn = jnp.where((lane == t) & (t <= q_id), st, sn)
        m_old = m_scr[...]
        m_fin = jnp.maximum(m_old, jnp.max(sn, axis=-1, keepdims=True))
        a_fin = jnp.exp(m_old - m_fin)
        pn = jnp.exp(sn - m_fin)
        l_fin = a_fin * l_scr[...] + jnp.sum(pn, axis=-1, keepdims=True)
        acc = a_fin * acc_scr[...]
        for t in range(dec_seq):
            pt = jnp.sum(jnp.where(lane == t, pn, 0.0), axis=-1, keepdims=True)
            acc = acc + pt * per_head(vn_ref, t)
        o = acc / l_fin
        for h in range(n_heads):
            oh = o[h * grp:h * grp + SUBLANES] - lam * o[h * grp + SUBLANES:(h + 1) * grp]
            o_ref[0, :, h * hw:(h + 1) * hw] = (_rms(oh, sub_ref[...]) * (1.0 - lam_init)).astype(BF16)


def _decode_queries(q_rows, dec_batch, dec_seq):
    d = q_rows.shape[1]
    qs = q_rows.reshape(dec_batch, dec_seq, d // (2 * HEAD_DIM), 2, HEAD_DIM)
    qs = jnp.pad(qs, ((0, 0), (0, SUBLANES - dec_seq), (0, 0), (0, 0), (0, 0)))
    q16 = jnp.einsum("bqhcd,ce->bcqhed", qs, jnp.eye(2, dtype=qs.dtype))
    return q16.reshape(dec_batch, 2 * SUBLANES, d)


def _decode_attn(q16, kn8, vn8, cache_k, cache_v, page_table, page, b_lam, subln, dec_seq, lam_init,
                 pages_per_step=8):
    dec_batch, _, width = q16.shape
    n_pages = page_table.shape[1]
    hw = 2 * HEAD_DIM
    n_heads = width // hw
    pps = pages_per_step if n_pages % pages_per_step == 0 else 1
    grp = 2 * SUBLANES
    rows = n_heads * grp
    col = jnp.arange(page * n_heads)
    spread = (col[None, :] // n_heads == jnp.arange(page)[:, None]).astype(BF16)
    headmask = (col[None, :] % n_heads == jnp.arange(rows)[:, None] // grp).astype(F32)

    def k_spec(u):
        return pl.BlockSpec((page * 2 * n_heads, HEAD_DIM),
                            lambda s, p, pt: (pt[s * n_pages + p * pps + u], 0))

    def v_spec(u):
        return pl.BlockSpec((page * n_heads, hw),
                            lambda s, p, pt: (pt[s * n_pages + p * pps + u], 0))

    def seq_spec(r):
        return pl.BlockSpec((1, r, width), lambda s, p, pt: (s, 0, 0))

    const = lambda a: pl.BlockSpec(a.shape, lambda s, p, pt: (0, 0))
    grid_spec = pltpu.PrefetchScalarGridSpec(
        num_scalar_prefetch=1,
        grid=(dec_batch, n_pages // pps),
        in_specs=[
            const(b_lam), const(subln), const(spread), const(headmask),
            seq_spec(grp), seq_spec(SUBLANES), seq_spec(SUBLANES),
        ] + [k_spec(u) for u in range(pps)] + [v_spec(u) for u in range(pps)],
        out_specs=seq_spec(SUBLANES),
        scratch_shapes=[
            pltpu.VMEM((rows, 1), F32),
            pltpu.VMEM((rows, 1), F32),
            pltpu.VMEM((rows, hw), F32),
        ],
    )
    return pl.pallas_call(
        functools.partial(_decode_attn_kernel, pages_per_step=pps, n_heads=n_heads, dec_seq=dec_seq,
                          lam_init=lam_init, scale=HEAD_DIM ** -0.5),
        grid_spec=grid_spec,
        out_shape=jax.ShapeDtypeStruct((dec_batch, SUBLANES, width), BF16),
        compiler_params=_params(("parallel", "arbitrary")),
        name="diff_attn_decode",
    )(page_table.reshape(-1), b_lam, subln, spread, headmask, q16, kn8, vn8,
      *([cache_k] * pps), *([cache_v] * pps))


def _wo_router_kernel(o_ref, x_ref, g_ref, wo_ref, wr_ref, br_ref, tri_ref, cnt0_ref,
                      x3_ref, hm_ref, meta_ref, cnt_ref, carry_scr, *, n_experts):
    i = pl.program_id(0)

    @pl.when(i == 0)
    def _():
        carry_scr[...] = cnt0_ref[...]

    x3 = x_ref[...] + _rms(_dot(o_ref[...], wo_ref[...]), g_ref[1:2, :])
    x3_ref[...] = x3
    hm = _rms(x3, g_ref[2:3, :])
    hm_ref[...] = hm
    lane = lax.broadcasted_iota(I32, (hm.shape[0], LANES), 1)
    logits = jnp.full((hm.shape[0], LANES), -jnp.inf, F32)
    for e in range(n_experts):
        col = jnp.sum(hm * wr_ref[e:e + 1, :], axis=-1, keepdims=True) + br_ref[0:1, e:e + 1]
        logits = jnp.where(lane == e, col, logits)
    m1 = jnp.max(logits, axis=-1, keepdims=True)
    i1 = jnp.min(jnp.where(logits == m1, lane, LANES), axis=-1, keepdims=True)
    rest = jnp.where(lane == i1, -jnp.inf, logits)
    m2 = jnp.max(rest, axis=-1, keepdims=True)
    i2 = jnp.min(jnp.where(rest == m2, lane, LANES), axis=-1, keepdims=True)
    e2 = jnp.exp(m2 - m1)
    g1 = 1.0 / (1.0 + e2)
    g2 = e2 / (1.0 + e2)
    oh1 = (lane == i1).astype(F32)
    oh2 = (lane == i2).astype(F32)
    tri = tri_ref[...]
    before1 = _dot(tri, oh1.astype(BF16))
    before2 = _dot(tri, oh2.astype(BF16))
    tot1 = jnp.sum(oh1, axis=0, keepdims=True)
    tot2 = jnp.sum(oh2, axis=0, keepdims=True)
    carry = carry_scr[...]
    r1 = jnp.sum(oh1 * (carry + before1), axis=-1, keepdims=True)
    r2 = jnp.sum(oh2 * (carry + tot1 + before2), axis=-1, keepdims=True)
    carry = carry + tot1 + tot2
    carry_scr[...] = carry
    cnt_ref[...] = carry
    meta = jnp.zeros(logits.shape, F32)
    for col, val in enumerate((i1.astype(F32), i2.astype(F32), g1, g2, r1, r2)):
        meta = jnp.where(lane == col, val, meta)
    meta_ref[...] = meta


def _wo_router(o, x, g4, w_o, w_r, b_r, cnt0, n_experts, tm_pref=TILE):
    n, d = x.shape
    tm = _tile(n, tm_pref)
    row = lambda i: (i, 0)
    tri = jnp.tril(jnp.ones((tm, tm), BF16), k=-1)
    return pl.pallas_call(
        functools.partial(_wo_router_kernel, n_experts=n_experts),
        grid=(n // tm,),
        in_specs=[
            pl.BlockSpec((tm, d), row),
            pl.BlockSpec((tm, d), row),
            _resident((4, d)),
            _resident(w_o.shape),
            _resident(w_r.shape),
            _resident(b_r.shape),
            _resident((tm, tm)),
            _resident((1, LANES)),
        ],
        out_specs=[
            pl.BlockSpec((tm, d), row),
            pl.BlockSpec((tm, d), row),
            pl.BlockSpec((tm, LANES), row),
            pl.BlockSpec((1, LANES), lambda i: (0, 0)),
        ],
        out_shape=[
            jax.ShapeDtypeStruct((n, d), F32),
            jax.ShapeDtypeStruct((n, d), F32),
            jax.ShapeDtypeStruct((n, LANES), F32),
            jax.ShapeDtypeStruct((1, LANES), F32),
        ],
        scratch_shapes=[pltpu.VMEM((1, LANES), F32)],
        compiler_params=_params(("arbitrary",)),
        name="wo_router",
    )(o, x, g4, w_o, w_r, b_r, tri, cnt0)


def _row_copy(src_ref, src_row, dst_ref, dst_row, sem):
    return pltpu.make_async_copy(src_ref.at[pl.ds(src_row, 1), :], dst_ref.at[pl.ds(dst_row, 1), :], sem)


def _dispatch_kernel(pos_ref, hm_ref, init_ref, xs_ref, sem):
    del init_ref
    tm = hm_ref.shape[0]
    base = pl.program_id(0) * tm * TOP_K

    def start(r, carry):
        for k in range(TOP_K):
            _row_copy(hm_ref, r, xs_ref, pos_ref[base + r * TOP_K + k], sem).start()
        return carry

    def wait(r, carry):
        for k in range(TOP_K):
            _row_copy(hm_ref, r, xs_ref, pos_ref[base + r * TOP_K + k], sem).wait()
        return carry

    lax.fori_loop(0, tm, start, 0, unroll=SUBLANES)
    lax.fori_loop(0, tm, wait, 0, unroll=SUBLANES)


def _dispatch(hm, pos_flat, xs, tm_pref=TILE):
    n, d = hm.shape
    tm = _tile(n, tm_pref)
    grid_spec = pltpu.PrefetchScalarGridSpec(
        num_scalar_prefetch=1,
        grid=(n // tm,),
        in_specs=[
            pl.BlockSpec((tm, d), lambda i, pos: (i, 0)),
            pl.BlockSpec(memory_space=pl.ANY),
        ],
        out_specs=pl.BlockSpec(memory_space=pl.ANY),
        scratch_shapes=[pltpu.SemaphoreType.DMA(())],
    )
    return pl.pallas_call(
        _dispatch_kernel,
        grid_spec=grid_spec,
        out_shape=jax.ShapeDtypeStruct(xs.shape, F32),
        input_output_aliases={2: 0},
        compiler_params=_params(("arbitrary",)),
        name="moe_dispatch",
    )(pos_flat, hm, xs)


def _combine_kernel(pos_ref, x_ref, g_ref, meta_ref, ys_ref, out_ref, a_scr, b_scr, sem):
    tm = x_ref.shape[0]
    base = pl.program_id(0) * tm * TOP_K
    bufs = (a_scr, b_scr)

    def start(r, carry):
        for k in range(TOP_K):
            _row_copy(ys_ref, pos_ref[base + r * TOP_K + k], bufs[k], r, sem).start()
        return carry

    def wait(r, carry):
        for k in range(TOP_K):
            _row_copy(ys_ref, pos_ref[base + r * TOP_K + k], bufs[k], r, sem).wait()
        return carry

    lax.fori_loop(0, tm, start, 0, unroll=SUBLANES)
    lax.fori_loop(0, tm, wait, 0, unroll=SUBLANES)
    meta = meta_ref[...]
    f = meta[:, 2:3] * a_scr[...] + meta[:, 3:4] * b_scr[...]
    out_ref[...] = x_ref[...] + _rms(f, g_ref[3:4, :])


def _combine(x, g4, meta, ys, pos_flat, tm_pref=TILE):
    n, d = x.shape
    tm = _tile(n, tm_pref)
    grid_spec = pltpu.PrefetchScalarGridSpec(
        num_scalar_prefetch=1,
        grid=(n // tm,),
        in_specs=[
            pl.BlockSpec((tm, d), lambda i, pos: (i, 0)),
            pl.BlockSpec((4, d), lambda i, pos: (0, 0)),
            pl.BlockSpec((tm, LANES), lambda i, pos: (i, 0)),
            pl.BlockSpec(memory_space=pl.ANY),
        ],
        out_specs=pl.BlockSpec((tm, d), lambda i, pos: (i, 0)),
        scratch_shapes=[pltpu.VMEM((tm, d), F32), pltpu.VMEM((tm, d), F32),
                        pltpu.SemaphoreType.DMA(())],
    )
    return pl.pallas_call(
        _combine_kernel,
        grid_spec=grid_spec,
        out_shape=jax.ShapeDtypeStruct((n, d), F32),
        compiler_params=_params(("arbitrary",)),
        name="moe_combine",
    )(pos_flat, x, g4, meta, ys)


def _rope_tables(pos):
    inv = jnp.exp(-math.log(ROPE_THETA) * jnp.arange(0, HEAD_DIM, 2, dtype=F32) / HEAD_DIM)
    ang = pos.astype(F32)[:, None] * inv[None, :]
    ang = jnp.concatenate([ang, ang], axis=-1)
    sign = jnp.where(jnp.arange(HEAD_DIM) < HEAD_DIM // 2, -1.0, 1.0).astype(F32)
    return jnp.cos(ang), jnp.sin(ang) * sign


def _mix_tables(w_s, b_s, dec_seq):
    causal = jnp.tril(jnp.ones((CHUNK, CHUNK), dtype=bool))
    w_prompt = jnp.where(causal[None], w_s, 0)
    per = CHUNK // dec_seq
    w_small = w_prompt[:, :dec_seq, :dec_seq]
    eye = jnp.eye(per, dtype=w_s.dtype)
    w_sample = jnp.einsum("ab,gts->gatbs", eye, w_small).reshape(N_GROUPS, CHUNK, CHUNK)
    gd = LANES
    b_prompt = jnp.repeat(jnp.transpose(b_s), gd, axis=1)
    b_sample = jnp.repeat(jnp.tile(jnp.transpose(b_s)[:dec_seq], (per, 1)), gd, axis=1)
    return (w_prompt.astype(BF16), b_prompt), (w_sample.astype(BF16), b_sample)


def kernel(x_prompt, x_sample, cache_k, cache_v, page_table, norm_g, a_w_in, a_ln_g, a_ln_b, a_w_s,
           a_b_s, a_w_out, f_w_gu, f_w_down, b_w_qkv, b_lam, b_subln, b_w_o, e_w_r, e_b_r, e_w_gu,
           e_w_down):
    batch, seq, d = x_prompt.shape
    dec_batch, dec_seq, _ = x_sample.shape
    assert norm_g.shape[0] == 2 and a_w_in.shape[0] == 1 and b_w_qkv.shape[0] == 1
    n_p, n_s = batch * seq, dec_batch * dec_seq
    n = n_p + n_s
    n_experts = e_w_r.shape[-1]
    n_heads = d // (2 * HEAD_DIM)
    page = cache_k.shape[2]
    past = page_table.shape[1] * page
    assert CHUNK % dec_seq == 0 and dec_seq <= SUBLANES
    tm_e = TILE
    xp, xs_ = x_prompt.reshape(n_p, d), x_sample.reshape(n_s, d)

    mix_p, mix_s = _mix_tables(a_w_s[0], a_b_s[0], dec_seq)
    w_in, w_out = a_w_in[0].astype(BF16), a_w_out[0].astype(BF16)
    w_gu, w_down = f_w_gu[0].astype(BF16), f_w_down[0].astype(BF16)

    def layer0(x, mix):
        x, v_rows = _gmlp_layer(x, norm_g[0], w_in, a_ln_g, a_ln_b, mix[0], mix[1], w_out)
        return _swiglu_layer(x, norm_g[0], w_gu, w_down), v_rows

    xp, _ = layer0(xp, mix_p)
    xs_, v_rows = layer0(xs_, mix_s)

    lam_init = 0.8 - 0.6 * math.exp(-0.3 * 1)
    w_qkv = b_w_qkv[0].astype(BF16)
    tab_rows = _tile(n_s, TILE)
    assert tab_rows % dec_seq == 0
    q_p, kb_p, vb_p, k_p, v_p = _qkv_layer(xp, norm_g[1], *_rope_tables(jnp.arange(seq)), w_qkv)
    q_s, kb_s, vb_s, k_s, v_s = _qkv_layer(
        xs_, norm_g[1], *_rope_tables(past + jnp.arange(tab_rows) % dec_seq), w_qkv)
    o_p = _prompt_attn(q_p, kb_p, vb_p, b_lam[0], b_subln, batch, seq, lam_init)
    pad8 = lambda a: jnp.pad(a.reshape(dec_batch, dec_seq, d).astype(F32),
                             ((0, 0), (0, SUBLANES - dec_seq), (0, 0)))
    o_s = _decode_attn(_decode_queries(q_s, dec_batch, dec_seq), pad8(kb_s), pad8(vb_s),
                       cache_k.reshape(-1, HEAD_DIM), cache_v.reshape(-1, 2 * HEAD_DIM),
                       page_table, page, b_lam[0], b_subln, dec_seq, lam_init)
    o_s = o_s[:, :dec_seq].reshape(n_s, d)

    w_o = b_w_o[0].astype(BF16)
    w_r = jnp.transpose(e_w_r[0])
    b_r = e_b_r
    cnt = jnp.zeros((1, LANES), F32)
    xp, hm_p, meta_p, cnt = _wo_router(o_p, xp, norm_g[1], w_o, w_r, b_r, cnt, n_experts)
    xs_, hm_s, meta_s, cnt = _wo_router(o_s, xs_, norm_g[1], w_o, w_r, b_r, cnt, n_experts)
    counts = cnt[0, :n_experts].astype(I32)
    tiles_per = (counts + tm_e - 1) // tm_e
    tile_end = jnp.cumsum(tiles_per)
    starts = (tile_end - tiles_per) * tm_e

    def slots(meta):
        expert = meta[:, 0:TOP_K].astype(I32)
        rank = meta[:, 4:4 + TOP_K].astype(I32)
        return (starts[expert] + rank).reshape(-1)

    pos_p, pos_s = slots(meta_p), slots(meta_s)
    n_tiles = (TOP_K * (n_p + n_s) + n_experts * (tm_e - 1)) // tm_e
    n_active = tile_end[-1:]
    tile_id = jnp.minimum(jnp.arange(n_tiles), n_active[0] - 1)
    tile_expert = jnp.sum(tile_id[:, None] >= tile_end[None, :], axis=1).astype(I32)
    xs = jnp.zeros((n_tiles * tm_e, d), F32)
    xs = _dispatch(hm_p, pos_p, xs)
    xs = _dispatch(hm_s, pos_s, xs)
    ys = _experts(xs, tile_expert, n_active.astype(I32), e_w_gu[0], e_w_down[0], tm_e)
    yp = _combine(xp, norm_g[1], meta_p, ys, pos_p)
    ysm = _combine(xs_, norm_g[1], meta_s, ys, pos_s)

    return (
        yp.reshape(batch, seq, d),
        ysm.reshape(dec_batch, dec_seq, d),
        k_p.reshape(1, batch, seq, n_heads, 2, HEAD_DIM),
        v_p.reshape(1, batch, seq, n_heads, 2 * HEAD_DIM),
        k_s.reshape(1, dec_batch, dec_seq, n_heads, 2, HEAD_DIM),
        v_s.reshape(1, dec_batch, dec_seq, n_heads, 2 * HEAD_DIM),
        v_rows.reshape(1, dec_batch, dec_seq, -1),
    )
```

```python
import functools
import math

import jax
import jax.numpy as jnp
from jax import lax
from jax.experimental import pallas as pl
from jax.experimental.pallas import tpu as pltpu

F32 = jnp.float32
BF16 = jnp.bfloat16
I32 = jnp.int32

EPS = 1e-6
CHUNK = 128
N_GROUPS = 16
HEAD_DIM = 128
ROPE_THETA = 10000.0
TOP_K = 2
NEG = -1e30
LANES = 128
SUBLANES = 8
VMEM_LIMIT = 56 * 1024 * 1024
TILE = 512


def _dot(a, b):
    return jnp.dot(a, b, preferred_element_type=F32)


def _dot_nt(a, b):
    return lax.dot_general(a, b, (((1,), (1,)), ((), ())), preferred_element_type=F32)


def _rms(x, g):
    return x * lax.rsqrt(jnp.mean(x * x, axis=-1, keepdims=True) + EPS) * g


def _tile(n, pref):
    t = min(n, pref)
    while n % t:
        t -= SUBLANES
    assert t > 0 and t % SUBLANES == 0, (n, pref)
    return t


def _params(sem, vmem=VMEM_LIMIT):
    return pltpu.CompilerParams(dimension_semantics=sem, vmem_limit_bytes=vmem)


def _resident(shape):
    return pl.BlockSpec(shape, lambda *_: (0,) * len(shape), pipeline_mode=pl.Buffered(1))


def _gmlp_kernel(x_ref, g_ref, win_ref, lng_ref, lnb_ref, wmix_ref, bias_ref, wout_ref,
                 out_ref, v_ref, h_scr, u_scr, vb_scr, y_scr, *, col_w):
    tm, d = x_ref.shape
    d_inner = u_scr.shape[1]
    x = x_ref[...]
    h_scr[...] = _rms(x, g_ref[0:1, :]).astype(BF16)
    n_col = d_inner // col_w
    s1 = jnp.zeros((tm, 1), F32)
    for c in range(n_col):
        cs = slice(c * col_w, (c + 1) * col_w)
        u_scr[:, cs] = jax.nn.gelu(_dot(h_scr[...], win_ref[:, cs]))
        v = jax.nn.gelu(_dot(h_scr[...], win_ref[:, d_inner + c * col_w:d_inner + (c + 1) * col_w]))
        v_ref[:, cs] = v
        s1 = s1 + jnp.sum(v, axis=-1, keepdims=True)
    mu = s1 / d_inner
    s2 = jnp.zeros((tm, 1), F32)
    for c in range(n_col):
        cs = slice(c * col_w, (c + 1) * col_w)
        dv = v_ref[:, cs] - mu
        s2 = s2 + jnp.sum(dv * dv, axis=-1, keepdims=True)
    rstd = lax.rsqrt(s2 / d_inner + EPS)
    for c in range(n_col):
        cs = slice(c * col_w, (c + 1) * col_w)
        vn = (v_ref[:, cs] - mu) * rstd * lng_ref[:, cs] + lnb_ref[:, cs]
        v_ref[:, cs] = vn
        vb_scr[:, cs] = vn.astype(BF16)
    gd = d_inner // N_GROUPS
    for r in range(tm // CHUNK):
        rs = slice(r * CHUNK, (r + 1) * CHUNK)
        for g in range(N_GROUPS):
            gs = slice(g * gd, (g + 1) * gd)
            z = _dot(wmix_ref[g], vb_scr[rs, gs]) + bias_ref[:, gs]
            y_scr[rs, gs] = (u_scr[rs, gs] * z).astype(BF16)
    m = _dot(y_scr[...], wout_ref[...])
    out_ref[...] = x + _rms(m, g_ref[1:2, :])


def _gmlp_layer(x, g4, w_in, ln_g, ln_b, wmix, bias, w_out):
    n, d = x.shape
    d_inner = w_out.shape[0]
    tm = _tile(n, 2 * CHUNK)
    assert tm % CHUNK == 0 and d_inner // N_GROUPS == LANES
    row = lambda i: (i, 0)
    return pl.pallas_call(
        functools.partial(_gmlp_kernel, col_w=_tile(d_inner, TILE)),
        grid=(n // tm,),
        in_specs=[
            pl.BlockSpec((tm, d), row),
            _resident((4, d)),
            _resident(w_in.shape),
            _resident((1, d_inner)),
            _resident((1, d_inner)),
            _resident(wmix.shape),
            _resident(bias.shape),
            _resident(w_out.shape),
        ],
        out_specs=[pl.BlockSpec((tm, d), row), pl.BlockSpec((tm, d_inner), row)],
        out_shape=[jax.ShapeDtypeStruct((n, d), F32), jax.ShapeDtypeStruct((n, d_inner), F32)],
        scratch_shapes=[
            pltpu.VMEM((tm, d), BF16),
            pltpu.VMEM((tm, d_inner), F32),
            pltpu.VMEM((tm, d_inner), BF16),
            pltpu.VMEM((tm, d_inner), BF16),
        ],
        compiler_params=_params(("parallel",)),
        name="gmlp_layer",
    )(x, g4, w_in, ln_g, ln_b, wmix, bias, w_out)


def _swiglu_step(h, wg_ref, wu_ref, wd_ref, acc_ref):
    g = _dot(h, wg_ref[...])
    u = _dot(h, wu_ref[...])
    a = (jax.nn.silu(g) * u).astype(BF16)
    acc_ref[...] += _dot(a, wd_ref[...])


def _rider_block(shape, row_steps, col_steps):
    rows, cols = shape
    if rows % row_steps or cols % col_steps:
        return None
    br, bc = rows // row_steps, cols // col_steps
    bf16_sublanes = 2 * SUBLANES
    if br % bf16_sublanes or bc % LANES or br * bc * 4 > 4 * 1024 * 1024:
        return None
    return br, bc


def _swiglu_kernel(x_ref, g_ref, wg_ref, wu_ref, wd_ref, *rest):
    if len(rest) == 5:
        cast_src_ref, out_ref, cast_dst_ref, h_scr, acc_scr = rest
        cast_dst_ref[...] = cast_src_ref[...].astype(BF16)
    else:
        out_ref, h_scr, acc_scr = rest
    f = pl.program_id(1)

    @pl.when(f == 0)
    def _():
        h_scr[...] = _rms(x_ref[...], g_ref[2:3, :]).astype(BF16)
        acc_scr[...] = jnp.zeros_like(acc_scr)

    _swiglu_step(h_scr[...], wg_ref, wu_ref, wd_ref, acc_scr)

    @pl.when(f == pl.num_programs(1) - 1)
    def _():
        out_ref[...] = x_ref[...] + _rms(acc_scr[...], g_ref[3:4, :])


def _swiglu_layer(x, g4, w_gu, w_down, cast_src=None, tm_pref=TILE, tf_pref=TILE):
    n, d = x.shape
    d_ff = w_down.shape[0]
    tm = _tile(n, tm_pref)
    tf = _tile(d_ff, tf_pref)
    nf = d_ff // tf
    in_specs = [
        pl.BlockSpec((tm, d), lambda i, f: (i, 0)),
        pl.BlockSpec((4, d), lambda i, f: (0, 0)),
        pl.BlockSpec((d, tf), lambda i, f: (0, f)),
        pl.BlockSpec((d, tf), lambda i, f: (0, f + nf)),
        pl.BlockSpec((tf, d), lambda i, f: (f, 0)),
    ]
    out_specs = [pl.BlockSpec((tm, d), lambda i, f: (i, 0))]
    out_shape = [jax.ShapeDtypeStruct((n, d), F32)]
    args = [x, g4, w_gu, w_gu, w_down]
    blk = None if cast_src is None else _rider_block(cast_src.shape, n // tm, nf)
    if blk is not None:
        in_specs.append(pl.BlockSpec(blk, lambda i, f: (i, f)))
        out_specs.append(pl.BlockSpec(blk, lambda i, f: (i, f)))
        out_shape.append(jax.ShapeDtypeStruct(cast_src.shape, BF16))
        args.append(cast_src)
    outs = pl.pallas_call(
        _swiglu_kernel,
        grid=(n // tm, nf),
        in_specs=in_specs,
        out_specs=out_specs,
        out_shape=out_shape,
        scratch_shapes=[pltpu.VMEM((tm, d), BF16), pltpu.VMEM((tm, d), F32)],
        compiler_params=_params(("parallel", "arbitrary")),
        name="swiglu_dense",
    )(*args)
    return outs[0], (outs[1] if blk is not None else None)


def _experts_kernel(te_ref, na_ref, xs_ref, wg_ref, wu_ref, wd_ref, ys_ref, h_scr):
    i = pl.program_id(0)
    f = pl.program_id(1)
    active = i < na_ref[0]

    @pl.when(f == 0)
    def _():
        h_scr[...] = xs_ref[...].astype(BF16)
        ys_ref[...] = jnp.zeros_like(ys_ref)

    @pl.when(active)
    def _():
        _swiglu_step(h_scr[...], wg_ref.at[0], wu_ref.at[0], wd_ref.at[0], ys_ref)


def _experts(xs, tile_expert, n_active, w_gu, w_down, tm, tf_pref=TILE):
    rows, d = xs.shape
    d_ff = w_down.shape[1]
    tf = _tile(d_ff, tf_pref)
    nf = d_ff // tf

    def fidx(i, f, na):
        return jnp.where(i < na[0], f, nf - 1)

    grid_spec = pltpu.PrefetchScalarGridSpec(
        num_scalar_prefetch=2,
        grid=(rows // tm, nf),
        in_specs=[
            pl.BlockSpec((tm, d), lambda i, f, te, na: (i, 0)),
            pl.BlockSpec((1, d, tf), lambda i, f, te, na: (te[i], 0, fidx(i, f, na))),
            pl.BlockSpec((1, d, tf), lambda i, f, te, na: (te[i], 0, fidx(i, f, na) + nf)),
            pl.BlockSpec((1, tf, d), lambda i, f, te, na: (te[i], fidx(i, f, na), 0)),
        ],
        out_specs=pl.BlockSpec((tm, d), lambda i, f, te, na: (i, 0)),
        scratch_shapes=[pltpu.VMEM((tm, d), BF16)],
    )
    return pl.pallas_call(
        _experts_kernel,
        grid_spec=grid_spec,
        out_shape=jax.ShapeDtypeStruct((rows, d), F32),
        compiler_params=_params(("arbitrary", "arbitrary")),
        name="swiglu_experts",
    )(tile_expert, n_active, xs, w_gu, w_gu, w_down)


def _rope(x, cos, sin_signed):
    parts = []
    for s in range(x.shape[1] // HEAD_DIM):
        xs = x[:, s * HEAD_DIM:(s + 1) * HEAD_DIM]
        parts.append(xs * cos + pltpu.roll(xs, HEAD_DIM // 2, axis=1) * sin_signed)
    return jnp.concatenate(parts, axis=1)


def _qkv_kernel(x_ref, g_ref, cos_ref, sin_ref, wq_ref, wk_ref, wv_ref,
                q_ref, kb_ref, vb_ref, k_ref, v_ref, h_scr):
    j = pl.program_id(1)

    @pl.when(j == 0)
    def _():
        h_scr[...] = _rms(x_ref[...], g_ref[0:1, :]).astype(BF16)

    h = h_scr[...]
    cos = cos_ref[...]
    sin = sin_ref[...]
    tm, tn = q_ref.shape
    n_heads, hw = v_ref.shape[1], v_ref.shape[2]
    q_ref[...] = _rope(_dot(h, wq_ref[...]), cos, sin).astype(BF16)
    k = _rope(_dot(h, wk_ref[...]), cos, sin)
    v = _dot(h, wv_ref[...])
    kb_ref[...] = k.astype(BF16)
    vb_ref[...] = v.astype(BF16)
    slabs = 2 * n_heads
    for jj in range(slabs * HEAD_DIM // tn):
        @pl.when(j == jj)
        def _():
            for s in range(tn // HEAD_DIM):
                k_ref[pl.ds(jj * (tn // HEAD_DIM) + s, tm, stride=slabs), :] = (
                    k[:, s * HEAD_DIM:(s + 1) * HEAD_DIM])
            for s in range(tn // hw):
                v_ref[:, jj * (tn // hw) + s, :] = v[:, s * hw:(s + 1) * hw]


def _qkv_layer(x, g4, cos_tab, sin_tab, w_qkv, tm_pref=TILE, tn_pref=TILE):
    n, d = x.shape
    width = w_qkv.shape[1] // 3
    tm = _tile(math.gcd(n, cos_tab.shape[0]), tm_pref)
    tn = _tile(width, tn_pref)
    nj = width // tn
    period = cos_tab.shape[0] // tm
    slabs = width // HEAD_DIM
    assert tn % (2 * HEAD_DIM) == 0
    tab = lambda i, j: (i % period, 0)
    blk = lambda i, j: (i, j)
    return pl.pallas_call(
        _qkv_kernel,
        grid=(n // tm, nj),
        in_specs=[
            pl.BlockSpec((tm, d), lambda i, j: (i, 0)),
            pl.BlockSpec((4, d), lambda i, j: (0, 0)),
            pl.BlockSpec((tm, HEAD_DIM), tab),
            pl.BlockSpec((tm, HEAD_DIM), tab),
            pl.BlockSpec((d, tn), lambda i, j: (0, j)),
            pl.BlockSpec((d, tn), lambda i, j: (0, j + nj)),
            pl.BlockSpec((d, tn), lambda i, j: (0, j + 2 * nj)),
        ],
        out_specs=[pl.BlockSpec((tm, tn), blk)] * 3 + [
            pl.BlockSpec((tm * slabs, HEAD_DIM), lambda i, j: (i, 0)),
            pl.BlockSpec((tm, slabs // 2, 2 * HEAD_DIM), lambda i, j: (i, 0, 0)),
        ],
        out_shape=[jax.ShapeDtypeStruct((n, width), BF16)] * 3 + [
            jax.ShapeDtypeStruct((n * slabs, HEAD_DIM), F32),
            jax.ShapeDtypeStruct((n, slabs // 2, 2 * HEAD_DIM), F32),
        ],
        scratch_shapes=[pltpu.VMEM((tm, d), BF16)],
        compiler_params=_params(("parallel", "arbitrary")),
        name="qkv_rope",
    )(x, g4, cos_tab, sin_tab, w_qkv, w_qkv, w_qkv)


def _lambda(lam_ref, lam_init):
    lp = lam_ref[...]
    a = jnp.sum(lp[0:1, :] * lp[1:2, :], axis=-1, keepdims=True)
    b = jnp.sum(lp[2:3, :] * lp[3:4, :], axis=-1, keepdims=True)
    return jnp.exp(a) - jnp.exp(b) + lam_init


def _prompt_attn_kernel(lam_ref, sub_ref, q_ref, k_ref, v_ref, *rest, lam_init, scale, n_q):
    if len(rest) == 3:
        cast_src_ref, o_ref, cast_dst_ref = rest
        cast_dst_ref[...] = cast_src_ref[...].astype(BF16)
    else:
        (o_ref,) = rest
    qi = pl.program_id(2)
    tq = q_ref.shape[0]

    def tile_body(t):
        w0 = t * tq
        outs = []
        for c in range(2):
            cs = slice(c * HEAD_DIM, (c + 1) * HEAD_DIM)
            q = q_ref[:, cs]
            sd = _dot_nt(q, k_ref[w0:w0 + tq, cs]) * scale
            r_id = lax.broadcasted_iota(I32, sd.shape, 0)
            c_id = lax.broadcasted_iota(I32, sd.shape, 1)
            sd = jnp.where(c_id <= r_id, sd, NEG)
            m = jnp.max(sd, axis=-1, keepdims=True)
            if t > 0:
                sm = _dot_nt(q, k_ref[0:w0, cs]) * scale
                m = jnp.maximum(m, jnp.max(sm, axis=-1, keepdims=True))
            pd = jnp.exp(sd - m)
            l = jnp.sum(pd, axis=-1, keepdims=True)
            acc = _dot(pd.astype(BF16), v_ref[w0:w0 + tq, :])
            if t > 0:
                pm = jnp.exp(sm - m)
                l = l + jnp.sum(pm, axis=-1, keepdims=True)
                acc = acc + _dot(pm.astype(BF16), v_ref[0:w0, :])
            outs.append(acc / l)
        lam = _lambda(lam_ref, lam_init)
        o = outs[0] - lam * outs[1]
        o_ref[...] = (_rms(o, sub_ref[...]) * (1.0 - lam_init)).astype(BF16)

    for t in range(n_q):
        pl.when(qi == t)(functools.partial(tile_body, t))


def _prompt_attn(q, k, v, b_lam, subln, batch, seq, lam_init, cast_src=None, tq_pref=TILE):
    n_heads = q.shape[1] // (2 * HEAD_DIM)
    hw = 2 * HEAD_DIM
    tq = _tile(seq, tq_pref)
    nq = seq // tq
    kv_spec = pl.BlockSpec((seq, hw), lambda b, h, i: (b, h))
    in_specs = [
        pl.BlockSpec((4, HEAD_DIM), lambda b, h, i: (0, 0)),
        pl.BlockSpec((1, hw), lambda b, h, i: (0, 0)),
        pl.BlockSpec((tq, hw), lambda b, h, i: (b * nq + i, h)),
        kv_spec,
        kv_spec,
    ]
    out_specs = [pl.BlockSpec((tq, hw), lambda b, h, i: (b * nq + i, h))]
    out_shape = [jax.ShapeDtypeStruct((batch * seq, n_heads * hw), BF16)]
    args = [b_lam, subln, q, k, v]
    blk = None if cast_src is None else _rider_block(cast_src.shape, batch * n_heads * nq, 1)
    if blk is not None:
        step = lambda b, h, i: ((b * n_heads + h) * nq + i, 0)
        in_specs.append(pl.BlockSpec(blk, step))
        out_specs.append(pl.BlockSpec(blk, step))
        out_shape.append(jax.ShapeDtypeStruct(cast_src.shape, BF16))
        args.append(cast_src)
    outs = pl.pallas_call(
        functools.partial(_prompt_attn_kernel, lam_init=lam_init, scale=HEAD_DIM ** -0.5, n_q=nq),
        grid=(batch, n_heads, nq),
        in_specs=in_specs,
        out_specs=out_specs,
        out_shape=out_shape,
        compiler_params=_params(("parallel", "parallel", "arbitrary")),
        name="diff_attn_prompt",
    )(*args)
    return outs[0], (outs[1] if blk is not None else None)


def _decode_attn_kernel(pt_ref, lam_ref, sub_ref, spread_ref, headmask_ref, q_ref, kn_ref, vn_ref,
                        *rest, pages_per_step, n_heads, dec_seq, lam_init, scale):
    k_refs = rest[:pages_per_step]
    v_refs = rest[pages_per_step:2 * pages_per_step]
    o_ref, m_scr, l_scr, acc_scr = rest[2 * pages_per_step:]
    del pt_ref
    step = pl.program_id(1)
    hw = 2 * HEAD_DIM
    grp = 2 * SUBLANES
    slabs = 2 * n_heads
    page = k_refs[0].shape[0] // slabs
    rows = n_heads * grp

    @pl.when(step == 0)
    def _():
        m_scr[...] = jnp.full_like(m_scr, NEG)
        l_scr[...] = jnp.zeros_like(l_scr)
        acc_scr[...] = jnp.zeros_like(acc_scr)

    def head_keys(k_ref, h):
        return jnp.concatenate([k_ref[pl.ds(2 * h + c, page, stride=slabs), :] for c in range(2)],
                               axis=1).astype(BF16)

    s = jnp.concatenate(
        [jnp.concatenate([_dot_nt(q_ref[0, :, h * hw:(h + 1) * hw], head_keys(k_ref, h))
                          for k_ref in k_refs], axis=1)
         for h in range(n_heads)], axis=0) * scale
    m_prev = m_scr[...]
    m_new = jnp.maximum(m_prev, jnp.max(s, axis=-1, keepdims=True))
    alpha = jnp.exp(m_prev - m_new)
    p = jnp.exp(s - m_new)
    l_scr[...] = alpha * l_scr[...] + jnp.sum(p, axis=-1, keepdims=True)
    pb = p.astype(BF16)
    pv = jnp.zeros(acc_scr.shape, F32)
    for u, v_ref in enumerate(v_refs):
        spread = _dot(pb[:, u * page:(u + 1) * page], spread_ref[...]) * headmask_ref[...]
        pv = pv + _dot(spread.astype(BF16), v_ref[...].astype(BF16))
    acc_scr[...] = alpha * acc_scr[...] + pv
    m_scr[...] = m_new

    @pl.when(step == pl.num_programs(1) - 1)
    def _():
        lam = _lambda(lam_ref, lam_init)
        lane = lax.broadcasted_iota(I32, (rows, LANES), 1)
        q_id = lax.broadcasted_iota(I32, (rows, LANES), 0) % SUBLANES
        qf = jnp.concatenate([q_ref[0, :, h * hw:(h + 1) * hw] for h in range(n_heads)],
                             axis=0).astype(F32)

        def per_head(ref, t):
            return jnp.concatenate(
                [jnp.broadcast_to(ref[0, t:t + 1, h * hw:(h + 1) * hw], (grp, hw))
                 for h in range(n_heads)], axis=0)

        sn = jnp.full((rows, LANES), NEG, F32)
        for t in range(dec_seq):
            st = jnp.sum(qf * per_head(kn_ref, t), axis=-1, keepdims=True) * scale
            sn = jnp.where((lane == t) & (t <= q_id), st, sn)
        m_old = m_scr[...]
        m_fin = jnp.maximum(m_old, jnp.max(sn, axis=-1, keepdims=True))
        a_fin = jnp.exp(m_old - m_fin)
        pn = jnp.exp(sn - m_fin)
        l_fin = a_fin * l_scr[...] + jnp.sum(pn, axis=-1, keepdims=True)
        acc = a_fin * acc_scr[...]
        for t in range(dec_seq):
            pt = jnp.sum(jnp.where(lane == t, pn, 0.0), axis=-1, keepdims=True)
            acc = acc + pt * per_head(vn_ref, t)
        o = acc / l_fin
        for h in range(n_heads):
            oh = o[h * grp:h * grp + SUBLANES] - lam * o[h * grp + SUBLANES:(h + 1) * grp]
            o_ref[0, :, h * hw:(h + 1) * hw] = (_rms(oh, sub_ref[...]) * (1.0 - lam_init)).astype(BF16)


def _decode_queries(q_rows, dec_batch, dec_seq):
    d = q_rows.shape[1]
    qs = q_rows.reshape(dec_batch, dec_seq, d // (2 * HEAD_DIM), 2, HEAD_DIM)
    qs = jnp.pad(qs, ((0, 0), (0, SUBLANES - dec_seq), (0, 0), (0, 0), (0, 0)))
    q16 = jnp.einsum("bqhcd,ce->bcqhed", qs, jnp.eye(2, dtype=qs.dtype))
    return q16.reshape(dec_batch, 2 * SUBLANES, d)


def _decode_attn(q16, kn8, vn8, cache_k, cache_v, page_table, page, b_lam, subln, dec_seq, lam_init,
                 pages_per_step=8):
    dec_batch, _, width = q16.shape
    n_pages = page_table.shape[1]
    hw = 2 * HEAD_DIM
    n_heads = width // hw
    pps = pages_per_step if n_pages % pages_per_step == 0 else 1
    grp = 2 * SUBLANES
    rows = n_heads * grp
    col = jnp.arange(page * n_heads)
    spread = (col[None, :] // n_heads == jnp.arange(page)[:, None]).astype(BF16)
    headmask = (col[None, :] % n_heads == jnp.arange(rows)[:, None] // grp).astype(F32)

    def k_spec(u):
        return pl.BlockSpec((page * 2 * n_heads, HEAD_DIM),
                            lambda s, p, pt: (pt[s * n_pages + p * pps + u], 0))

    def v_spec(u):
        return pl.BlockSpec((page * n_heads, hw),
                            lambda s, p, pt: (pt[s * n_pages + p * pps + u], 0))

    def seq_spec(r):
        return pl.BlockSpec((1, r, width), lambda s, p, pt: (s, 0, 0))

    const = lambda a: pl.BlockSpec(a.shape, lambda s, p, pt: (0, 0))
    grid_spec = pltpu.PrefetchScalarGridSpec(
        num_scalar_prefetch=1,
        grid=(dec_batch, n_pages // pps),
        in_specs=[
            const(b_lam), const(subln), const(spread), const(headmask),
            seq_spec(grp), seq_spec(SUBLANES), seq_spec(SUBLANES),
        ] + [k_spec(u) for u in range(pps)] + [v_spec(u) for u in range(pps)],
        out_specs=seq_spec(SUBLANES),
        scratch_shapes=[
            pltpu.VMEM((rows, 1), F32),
            pltpu.VMEM((rows, 1), F32),
            pltpu.VMEM((rows, hw), F32),
        ],
    )
    return pl.pallas_call(
        functools.partial(_decode_attn_kernel, pages_per_step=pps, n_heads=n_heads, dec_seq=dec_seq,
                          lam_init=lam_init, scale=HEAD_DIM ** -0.5),
        grid_spec=grid_spec,
        out_shape=jax.ShapeDtypeStruct((dec_batch, SUBLANES, width), BF16),
        compiler_params=_params(("parallel", "arbitrary")),
        name="diff_attn_decode",
    )(page_table.reshape(-1), b_lam, subln, spread, headmask, q16, kn8, vn8,
      *([cache_k] * pps), *([cache_v] * pps))


def _wo_router_kernel(o_ref, x_ref, g_ref, wo_ref, wr_ref, br_ref, tri_ref, cnt0_ref,
                      x3_ref, hm_ref, meta_ref, cnt_ref, carry_scr, *, n_experts):
    i = pl.program_id(0)

    @pl.when(i == 0)
    def _():
        carry_scr[...] = cnt0_ref[...]

    x3 = x_ref[...] + _rms(_dot(o_ref[...], wo_ref[...]), g_ref[1:2, :])
    x3_ref[...] = x3
    hm = _rms(x3, g_ref[2:3, :])
    hm_ref[...] = hm
    lane = lax.broadcasted_iota(I32, (hm.shape[0], LANES), 1)
    logits = jnp.full((hm.shape[0], LANES), -jnp.inf, F32)
    for e in range(n_experts):
        col = jnp.sum(hm * wr_ref[e:e + 1, :], axis=-1, keepdims=True) + br_ref[0:1, e:e + 1]
        logits = jnp.where(lane == e, col, logits)
    m1 = jnp.max(logits, axis=-1, keepdims=True)
    i1 = jnp.min(jnp.where(logits == m1, lane, LANES), axis=-1, keepdims=True)
    rest = jnp.where(lane == i1, -jnp.inf, logits)
    m2 = jnp.max(rest, axis=-1, keepdims=True)
    i2 = jnp.min(jnp.where(rest == m2, lane, LANES), axis=-1, keepdims=True)
    e2 = jnp.exp(m2 - m1)
    g1 = 1.0 / (1.0 + e2)
    g2 = e2 / (1.0 + e2)
    oh1 = (lane == i1).astype(F32)
    oh2 = (lane == i2).astype(F32)
    tri = tri_ref[...]
    before1 = _dot(tri, oh1.astype(BF16))
    before2 = _dot(tri, oh2.astype(BF16))
    tot1 = jnp.sum(oh1, axis=0, keepdims=True)
    tot2 = jnp.sum(oh2, axis=0, keepdims=True)
    carry = carry_scr[...]
    r1 = jnp.sum(oh1 * (carry + before1), axis=-1, keepdims=True)
    r2 = jnp.sum(oh2 * (carry + tot1 + before2), axis=-1, keepdims=True)
    carry = carry + tot1 + tot2
    carry_scr[...] = carry
    cnt_ref[...] = carry
    meta = jnp.zeros(logits.shape, F32)
    for col, val in enumerate((i1.astype(F32), i2.astype(F32), g1, g2, r1, r2)):
        meta = jnp.where(lane == col, val, meta)
    meta_ref[...] = meta


def _wo_router(o, x, g4, w_o, w_r, b_r, cnt0, n_experts, tm_pref=TILE):
    n, d = x.shape
    tm = _tile(n, tm_pref)
    row = lambda i: (i, 0)
    tri = jnp.tril(jnp.ones((tm, tm), BF16), k=-1)
    return pl.pallas_call(
        functools.partial(_wo_router_kernel, n_experts=n_experts),
        grid=(n // tm,),
        in_specs=[
            pl.BlockSpec((tm, d), row),
            pl.BlockSpec((tm, d), row),
            _resident((4, d)),
            _resident(w_o.shape),
            _resident(w_r.shape),
            _resident(b_r.shape),
            _resident((tm, tm)),
            _resident((1, LANES)),
        ],
        out_specs=[
            pl.BlockSpec((tm, d), row),
            pl.BlockSpec((tm, d), row),
            pl.BlockSpec((tm, LANES), row),
            pl.BlockSpec((1, LANES), lambda i: (0, 0)),
        ],
        out_shape=[
            jax.ShapeDtypeStruct((n, d), F32),
            jax.ShapeDtypeStruct((n, d), F32),
            jax.ShapeDtypeStruct((n, LANES), F32),
            jax.ShapeDtypeStruct((1, LANES), F32),
        ],
        scratch_shapes=[pltpu.VMEM((1, LANES), F32)],
        compiler_params=_params(("arbitrary",)),
        name="wo_router",
    )(o, x, g4, w_o, w_r, b_r, tri, cnt0)


def _row_copy(src_ref, src_row, dst_ref, dst_row, sem):
    return pltpu.make_async_copy(src_ref.at[pl.ds(src_row, 1), :], dst_ref.at[pl.ds(dst_row, 1), :], sem)


def _dispatch_kernel(pos_ref, hm_ref, init_ref, xs_ref, sem):
    del init_ref
    tm = hm_ref.shape[0]
    base = pl.program_id(0) * tm * TOP_K

    def start(r, carry):
        for k in range(TOP_K):
            _row_copy(hm_ref, r, xs_ref, pos_ref[base + r * TOP_K + k], sem).start()
        return carry

    def wait(r, carry):
        for k in range(TOP_K):
            _row_copy(hm_ref, r, xs_ref, pos_ref[base + r * TOP_K + k], sem).wait()
        return carry

    lax.fori_loop(0, tm, start, 0, unroll=SUBLANES)
    lax.fori_loop(0, tm, wait, 0, unroll=SUBLANES)


def _dispatch(hm, pos_flat, xs, tm_pref=TILE):
    n, d = hm.shape
    tm = _tile(n, tm_pref)
    grid_spec = pltpu.PrefetchScalarGridSpec(
        num_scalar_prefetch=1,
        grid=(n // tm,),
        in_specs=[
            pl.BlockSpec((tm, d), lambda i, pos: (i, 0)),
            pl.BlockSpec(memory_space=pl.ANY),
        ],
        out_specs=pl.BlockSpec(memory_space=pl.ANY),
        scratch_shapes=[pltpu.SemaphoreType.DMA(())],
    )
    return pl.pallas_call(
        _dispatch_kernel,
        grid_spec=grid_spec,
        out_shape=jax.ShapeDtypeStruct(xs.shape, F32),
        input_output_aliases={2: 0},
        compiler_params=_params(("arbitrary",)),
        name="moe_dispatch",
    )(pos_flat, hm, xs)


def _combine_kernel(pos_ref, x_ref, g_ref, meta_ref, ys_ref, out_ref, a_scr, b_scr, sem):
    tm = x_ref.shape[0]
    base = pl.program_id(0) * tm * TOP_K
    bufs = (a_scr, b_scr)

    def start(r, carry):
        for k in range(TOP_K):
            _row_copy(ys_ref, pos_ref[base + r * TOP_K + k], bufs[k], r, sem).start()
        return carry

    def wait(r, carry):
        for k in range(TOP_K):
            _row_copy(ys_ref, pos_ref[base + r * TOP_K + k], bufs[k], r, sem).wait()
        return carry

    lax.fori_loop(0, tm, start, 0, unroll=SUBLANES)
    lax.fori_loop(0, tm, wait, 0, unroll=SUBLANES)
    meta = meta_ref[...]
    f = meta[:, 2:3] * a_scr[...] + meta[:, 3:4] * b_scr[...]
    out_ref[...] = x_ref[...] + _rms(f, g_ref[3:4, :])


def _combine(x, g4, meta, ys, pos_flat, tm_pref=TILE):
    n, d = x.shape
    tm = _tile(n, tm_pref)
    grid_spec = pltpu.PrefetchScalarGridSpec(
        num_scalar_prefetch=1,
        grid=(n // tm,),
        in_specs=[
            pl.BlockSpec((tm, d), lambda i, pos: (i, 0)),
            pl.BlockSpec((4, d), lambda i, pos: (0, 0)),
            pl.BlockSpec((tm, LANES), lambda i, pos: (i, 0)),
            pl.BlockSpec(memory_space=pl.ANY),
        ],
        out_specs=pl.BlockSpec((tm, d), lambda i, pos: (i, 0)),
        scratch_shapes=[pltpu.VMEM((tm, d), F32), pltpu.VMEM((tm, d), F32),
                        pltpu.SemaphoreType.DMA(())],
    )
    return pl.pallas_call(
        _combine_kernel,
        grid_spec=grid_spec,
        out_shape=jax.ShapeDtypeStruct((n, d), F32),
        compiler_params=_params(("arbitrary",)),
        name="moe_combine",
    )(pos_flat, x, g4, meta, ys)


def _rope_tables(pos):
    inv = jnp.exp(-math.log(ROPE_THETA) * jnp.arange(0, HEAD_DIM, 2, dtype=F32) / HEAD_DIM)
    ang = pos.astype(F32)[:, None] * inv[None, :]
    ang = jnp.concatenate([ang, ang], axis=-1)
    sign = jnp.where(jnp.arange(HEAD_DIM) < HEAD_DIM // 2, -1.0, 1.0).astype(F32)
    return jnp.cos(ang), jnp.sin(ang) * sign


def _mix_tables(w_s, b_s, dec_seq):
    causal = jnp.tril(jnp.ones((CHUNK, CHUNK), dtype=bool))
    w_prompt = jnp.where(causal[None], w_s, 0)
    per = CHUNK // dec_seq
    w_small = w_prompt[:, :dec_seq, :dec_seq]
    eye = jnp.eye(per, dtype=w_s.dtype)
    w_sample = jnp.einsum("ab,gts->gatbs", eye, w_small).reshape(N_GROUPS, CHUNK, CHUNK)
    gd = LANES
    b_prompt = jnp.repeat(jnp.transpose(b_s), gd, axis=1)
    b_sample = jnp.repeat(jnp.tile(jnp.transpose(b_s)[:dec_seq], (per, 1)), gd, axis=1)
    return (w_prompt.astype(BF16), b_prompt), (w_sample.astype(BF16), b_sample)


def kernel(x_prompt, x_sample, cache_k, cache_v, page_table, norm_g, a_w_in, a_ln_g, a_ln_b, a_w_s,
           a_b_s, a_w_out, f_w_gu, f_w_down, b_w_qkv, b_lam, b_subln, b_w_o, e_w_r, e_b_r, e_w_gu,
           e_w_down):
    batch, seq, d = x_prompt.shape
    dec_batch, dec_seq, _ = x_sample.shape
    assert norm_g.shape[0] == 2 and a_w_in.shape[0] == 1 and b_w_qkv.shape[0] == 1
    n_p, n_s = batch * seq, dec_batch * dec_seq
    n = n_p + n_s
    n_experts = e_w_r.shape[-1]
    n_heads = d // (2 * HEAD_DIM)
    page = cache_k.shape[2]
    past = page_table.shape[1] * page
    assert CHUNK % dec_seq == 0 and dec_seq <= SUBLANES
    tm_e = TILE
    xp, xs_ = x_prompt.reshape(n_p, d), x_sample.reshape(n_s, d)

    mix_p, mix_s = _mix_tables(a_w_s[0], a_b_s[0], dec_seq)
    w_in, w_out = a_w_in[0].astype(BF16), a_w_out[0].astype(BF16)
    w_gu, w_down = f_w_gu[0].astype(BF16), f_w_down[0].astype(BF16)

    def layer0(x, mix, cast_src=None):
        x, v_rows = _gmlp_layer(x, norm_g[0], w_in, a_ln_g, a_ln_b, mix[0], mix[1], w_out)
        x, cast = _swiglu_layer(x, norm_g[0], w_gu, w_down, cast_src)
        return x, v_rows, cast

    e_gu_shape, e_down_shape = e_w_gu.shape[1:], e_w_down.shape[1:]
    xp, _, e_gu = layer0(xp, mix_p, e_w_gu[0].reshape(-1, e_gu_shape[-1]))
    e_gu = e_w_gu[0].astype(BF16) if e_gu is None else e_gu.reshape(e_gu_shape)
    xs_, v_rows, _ = layer0(xs_, mix_s)

    lam_init = 0.8 - 0.6 * math.exp(-0.3 * 1)
    w_qkv = b_w_qkv[0].astype(BF16)
    tab_rows = _tile(n_s, TILE)
    assert tab_rows % dec_seq == 0
    q_p, kb_p, vb_p, k_p, v_p = _qkv_layer(xp, norm_g[1], *_rope_tables(jnp.arange(seq)), w_qkv)
    q_s, kb_s, vb_s, k_s, v_s = _qkv_layer(
        xs_, norm_g[1], *_rope_tables(past + jnp.arange(tab_rows) % dec_seq), w_qkv)
    o_p, e_down = _prompt_attn(q_p, kb_p, vb_p, b_lam[0], b_subln, batch, seq, lam_init,
                               e_w_down[0].reshape(-1, e_down_shape[-1]))
    e_down = e_w_down[0].astype(BF16) if e_down is None else e_down.reshape(e_down_shape)
    pad8 = lambda a: jnp.pad(a.reshape(dec_batch, dec_seq, d).astype(F32),
                             ((0, 0), (0, SUBLANES - dec_seq), (0, 0)))
    o_s = _decode_attn(_decode_queries(q_s, dec_batch, dec_seq), pad8(kb_s), pad8(vb_s),
                       cache_k.reshape(-1, HEAD_DIM), cache_v.reshape(-1, 2 * HEAD_DIM),
                       page_table, page, b_lam[0], b_subln, dec_seq, lam_init)
    o_s = o_s[:, :dec_seq].reshape(n_s, d)

    w_o = b_w_o[0].astype(BF16)
    w_r = jnp.transpose(e_w_r[0])
    b_r = e_b_r
    cnt = jnp.zeros((1, LANES), F32)
    xp, hm_p, meta_p, cnt = _wo_router(o_p, xp, norm_g[1], w_o, w_r, b_r, cnt, n_experts)
    xs_, hm_s, meta_s, cnt = _wo_router(o_s, xs_, norm_g[1], w_o, w_r, b_r, cnt, n_experts)
    counts = cnt[0, :n_experts].astype(I32)
    tiles_per = (counts + tm_e - 1) // tm_e
    tile_end = jnp.cumsum(tiles_per)
    starts = (tile_end - tiles_per) * tm_e

    def slots(meta):
        expert = meta[:, 0:TOP_K].astype(I32)
        rank = meta[:, 4:4 + TOP_K].astype(I32)
        return (starts[expert] + rank).reshape(-1)

    pos_p, pos_s = slots(meta_p), slots(meta_s)
    n_tiles = (TOP_K * (n_p + n_s) + n_experts * (tm_e - 1)) // tm_e
    n_active = tile_end[-1:]
    tile_id = jnp.minimum(jnp.arange(n_tiles), n_active[0] - 1)
    tile_expert = jnp.sum(tile_id[:, None] >= tile_end[None, :], axis=1).astype(I32)
    xs = jnp.zeros((n_tiles * tm_e, d), F32)
    xs = _dispatch(hm_p, pos_p, xs)
    xs = _dispatch(hm_s, pos_s, xs)
    ys = _experts(xs, tile_expert, n_active.astype(I32), e_gu, e_down, tm_e)
    yp = _combine(xp, norm_g[1], meta_p, ys, pos_p)
    ysm = _combine(xs_, norm_g[1], meta_s, ys, pos_s)

    return (
        yp.reshape(batch, seq, d),
        ysm.reshape(dec_batch, dec_seq, d),
        k_p.reshape(1, batch, seq, n_heads, 2, HEAD_DIM),
        v_p.reshape(1, batch, seq, n_heads, 2 * HEAD_DIM),
        k_s.reshape(1, dec_batch, dec_seq, n_heads, 2, HEAD_DIM),
        v_s.reshape(1, dec_batch, dec_seq, n_heads, 2 * HEAD_DIM),
        v_rows.reshape(1, dec_batch, dec_seq, -1),
    )
```
